```python
import jax, jax.numpy as jnp
from jax import lax
import numpy as np

D_MODEL = 1024
BATCH = 32
SEQ = 2048
DEPTH = 1
DEC_BATCH = 128
DEC_SEQ = 8
PAST_LEN = 16384
PAGE_SIZE = 128

GDN_HEADS = 8
GDN_DK = 64
GDN_DV = 64
GDN_CONV = 4
GDN_CHUNK = 64
GDN_CONV_DIM = GDN_HEADS * (2 * GDN_DK + GDN_DV)
MLA_HEADS = 8
MLA_Q_LORA = 256
MLA_KV_LORA = 128
MLA_NOPE = 64
MLA_ROPE = 32
MLA_V = 64
MLA_SCALE = (MLA_NOPE + MLA_ROPE) ** -0.5
ROPE_THETA = 10000.0
Q_BLOCK = 128
GROUP_COLS = (GDN_CONV_DIM, GDN_HEADS * GDN_DV, GDN_HEADS, GDN_HEADS,
              MLA_Q_LORA, MLA_KV_LORA, MLA_ROPE)
IN_COLS = sum(GROUP_COLS)
MIX_WIDTH = GDN_HEADS * GDN_DV + MLA_HEADS * MLA_V
D_FF = 2816
FFN_CONV = 3
ALPHA = (2.0 * DEPTH) ** 0.25
BETA_INIT = (8.0 * DEPTH) ** -0.25
RMS_EPS = 1e-6
LN_EPS = 1e-5

kernel_name = "hymba_gdn_mla_convffn_deepnorm_step"


def split_cols(x, widths):
    idx, off = [], 0
    for w in widths[:-1]:
        off += w
        idx.append(off)
    return jnp.split(x, idx, axis=-1)


def rmsnorm(x, w):
    xf = x.astype(jnp.float32)
    y = xf * lax.rsqrt(jnp.mean(xf * xf, axis=-1, keepdims=True) + RMS_EPS)
    return (y * w.astype(jnp.float32)).astype(x.dtype)


def layernorm(x, g, b):
    xf = x.astype(jnp.float32)
    mu = jnp.mean(xf, axis=-1, keepdims=True)
    var = jnp.mean(jnp.square(xf - mu), axis=-1, keepdims=True)
    y = (xf - mu) * lax.rsqrt(var + LN_EPS) * g.astype(jnp.float32) + b.astype(jnp.float32)
    return y.astype(x.dtype)


def l2norm(x):
    return x * lax.rsqrt(jnp.sum(x * x, axis=-1, keepdims=True) + 1e-6)


def causal_dwconv(x_hist, w):
    width = w.shape[0]
    T = x_hist.shape[1] - width + 1
    y = x_hist[:, 0:T] * w[0]
    for j in range(1, width):
        y = y + x_hist[:, j:j + T] * w[j]
    return y


def rope(x, pos):
    half = MLA_ROPE // 2
    inv = ROPE_THETA ** (-jnp.arange(half, dtype=jnp.float32) / half)
    ang = pos.astype(jnp.float32)[:, None] * inv
    ang = ang.reshape((ang.shape[0],) + (1,) * (x.ndim - 3) + (half,))
    cos, sin = jnp.cos(ang), jnp.sin(ang)
    xf = x.astype(jnp.float32)
    x1, x2 = xf[..., :half], xf[..., half:]
    return jnp.concatenate([x1 * cos - x2 * sin, x1 * sin + x2 * cos], axis=-1).astype(x.dtype)


def gated_delta_chunked(q, k, v, g, beta, S0):
    f32 = jnp.float32
    B, T, H, DK = q.shape
    DV = v.shape[-1]
    C = GDN_CHUNK
    N = -(-T // C)
    pad = N * C - T
    q = l2norm(q.astype(f32)) * (DK ** -0.5)
    k = l2norm(k.astype(f32))

    def chunks(x):
        x = jnp.pad(x.astype(f32), [(0, 0), (0, pad)] + [(0, 0)] * (x.ndim - 2))
        x = x.reshape((B, N, C) + x.shape[2:])
        return jnp.transpose(x, (1, 0, 3, 2) + tuple(range(4, x.ndim)))

    qc, kc, vc, gc, bc = chunks(q), chunks(k), chunks(v), chunks(g), chunks(beta)
    gc = jnp.cumsum(gc, axis=-1)
    idx = jnp.arange(C)
    incl = idx[:, None] >= idx[None, :]
    strict = idx[:, None] > idx[None, :]
    decay = jnp.exp(jnp.where(incl, gc[..., :, None] - gc[..., None, :], -jnp.inf))
    kb = kc * bc[..., None]
    L = jnp.where(strict, jnp.einsum('nbhcd,nbhsd->nbhcs', kb, kc) * decay, 0.0)
    eye = jnp.eye(C, dtype=f32)
    Tm = lax.linalg.triangular_solve(eye + L, jnp.broadcast_to(eye, L.shape),
                                     left_side=True, lower=True)
    u = jnp.einsum('nbhcs,nbhsv->nbhcv', Tm, vc * bc[..., None])
    w = jnp.einsum('nbhcs,nbhsd->nbhcd', Tm, kb * jnp.exp(gc)[..., None])
    A = jnp.einsum('nbhcd,nbhsd->nbhcs', qc, kc) * decay
    qg = qc * jnp.exp(gc)[..., None]
    kg = kc * jnp.exp(gc[..., -1:] - gc)[..., None]
    g_last = jnp.exp(gc[..., -1])

    def step(S, xs):
        u_n, w_n, A_n, qg_n, kg_n, gl_n = xs
        v_new = u_n - jnp.einsum('bhcd,bhdv->bhcv', w_n, S)
        o = jnp.einsum('bhcd,bhdv->bhcv', qg_n, S) + jnp.einsum('bhcs,bhsv->bhcv', A_n, v_new)
        S = S * gl_n[..., None, None] + jnp.einsum('bhcd,bhcv->bhdv', kg_n, v_new)
        return S, o

    S, o = lax.scan(step, S0.astype(f32), (u, w, A, qg, kg, g_last))
    o = jnp.transpose(o, (1, 0, 3, 2, 4)).reshape(B, N * C, H, DV)[:, :T]
    return o, S


def mla_attention(q_lat, q_pe, q_pos, c_all, kpe_all, k_pos):
    B, H, T, R = q_lat.shape
    QB = Q_BLOCK if T % Q_BLOCK == 0 else T
    NB = T // QB

    def blocks(a):
        return jnp.moveaxis(a.reshape(B, H, NB, QB, a.shape[-1]), 2, 0)

    def one_block(args):
        ql, qp, qpos = args
        s = (jnp.einsum('bhqr,bkr->bhqk', ql, c_all)
             + jnp.einsum('bhqp,bkp->bhqk', qp, kpe_all)).astype(jnp.float32) * MLA_SCALE
        s = jnp.where(k_pos[None, :] <= qpos[:, None], s, -jnp.inf)
        p = jax.nn.softmax(s, axis=-1).astype(c_all.dtype)
        return jnp.einsum('bhqk,bkr->bhqr', p, c_all)

    o = lax.map(one_block, (blocks(q_lat), blocks(q_pe), q_pos.reshape(NB, QB)))
    return jnp.moveaxis(o, 0, 2).reshape(B, H, T, R)


def hybrid_layer(x, past_ckv, past_kpe, S0, conv_hist, ffn_hist,
                 w_in, gdn_conv_w, gdn_A_log, gdn_dt_bias, gdn_norm_w,
                 mla_q_norm_w, mla_w_uq, mla_kv_norm_w, mla_w_uk, mla_w_uv,
                 w_out, ln1_g, ln1_b, ffn_w_up, ffn_conv_w, ffn_w_down, ln2_g, ln2_b):
    B, T, _ = x.shape
    f32 = jnp.float32
    pos = past_ckv.shape[1] + jnp.arange(T, dtype=jnp.int32)
    proj = x @ w_in
    qkv_raw, z, b_raw, a_raw, cq, ckv_raw, kpe_raw = split_cols(proj, GROUP_COLS)

    xh = jnp.concatenate([conv_hist, qkv_raw], axis=1)
    qkv = jax.nn.silu(causal_dwconv(xh, gdn_conv_w))
    new_conv = xh[:, -(GDN_CONV - 1):]
    q, k, v = split_cols(qkv, (GDN_HEADS * GDN_DK, GDN_HEADS * GDN_DK, GDN_HEADS * GDN_DV))
    q = q.reshape(B, T, GDN_HEADS, GDN_DK)
    k = k.reshape(B, T, GDN_HEADS, GDN_DK)
    v = v.reshape(B, T, GDN_HEADS, GDN_DV)
    beta = jax.nn.sigmoid(b_raw.astype(f32))
    g = -jnp.exp(gdn_A_log.astype(f32)) * jax.nn.softplus(a_raw.astype(f32) + gdn_dt_bias.astype(f32))
    o_gdn, S_new = gated_delta_chunked(q, k, v, g, beta, S0)
    o_gdn = rmsnorm(o_gdn.astype(x.dtype), gdn_norm_w) * jax.nn.silu(z.reshape(B, T, GDN_HEADS, GDN_DV))
    o_gdn = o_gdn.reshape(B, T, GDN_HEADS * GDN_DV)

    qh = (rmsnorm(cq, mla_q_norm_w) @ mla_w_uq).reshape(B, T, MLA_HEADS, MLA_NOPE + MLA_ROPE)
    q_nope = qh[..., :MLA_NOPE]
    q_pe = rope(qh[..., MLA_NOPE:], pos)
    c_new = rmsnorm(ckv_raw, mla_kv_norm_w)
    kpe_new = rope(kpe_raw, pos)
    c_all = jnp.concatenate([past_ckv, c_new], axis=1)
    kpe_all = jnp.concatenate([past_kpe, kpe_new], axis=1)
    k_pos = jnp.arange(c_all.shape[1], dtype=jnp.int32)
    q_lat = jnp.einsum('bthn,rhn->bhtr', q_nope, mla_w_uk)
    o_lat = mla_attention(q_lat, jnp.transpose(q_pe, (0, 2, 1, 3)), pos, c_all, kpe_all, k_pos)
    o_mla = jnp.einsum('bhtr,rhv->bthv', o_lat, mla_w_uv).reshape(B, T, MLA_HEADS * MLA_V)

    mix = jnp.concatenate([o_gdn, o_mla], axis=-1) @ w_out
    x1 = layernorm(ALPHA * x + mix, ln1_g, ln1_b)

    up = x1 @ ffn_w_up
    uh = jnp.concatenate([ffn_hist, up], axis=1)
    hc = causal_dwconv(uh, ffn_conv_w)
    new_ffn = uh[:, -(FFN_CONV - 1):]
    u_half, g_half = jnp.split(hc, 2, axis=-1)
    y = layernorm(ALPHA * x1 + (jax.nn.silu(g_half) * u_half) @ ffn_w_down, ln2_g, ln2_b)
    return y, c_new, kpe_new, S_new.astype(S0.dtype), new_conv, new_ffn


def setup_inputs(seed: int = 0) -> dict:
    key = jax.random.key(seed)
    ks = jax.random.split(key, 32)
    f32 = jnp.float32
    n_pages = PAST_LEN // PAGE_SIZE
    n_pool = (5 * DEC_BATCH * n_pages) // 4

    def nrm(k, shape, scale):
        return jax.random.normal(k, shape, f32) * scale

    def gain(k, n):
        return 1.0 + nrm(k, (DEPTH, n), 0.01)

    page_table = jax.random.permutation(ks[4], n_pool)[:DEC_BATCH * n_pages]
    page_table = page_table.reshape(DEC_BATCH, n_pages).astype(jnp.int32)
    return {
        "x_prompt": nrm(ks[0], (BATCH, SEQ, D_MODEL), 1.0),
        "x_sample": nrm(ks[1], (DEC_BATCH, DEC_SEQ, D_MODEL), 1.0),
        "cache_ckv": nrm(ks[2], (DEPTH, n_pool, PAGE_SIZE, MLA_KV_LORA), 1.0),
        "cache_kpe": nrm(ks[3], (DEPTH, n_pool, PAGE_SIZE, MLA_ROPE), 1.0),
        "page_table": page_table,
        "state_gdn": nrm(ks[5], (DEPTH, DEC_BATCH, GDN_HEADS, GDN_DK, GDN_DV), 0.1),
        "state_gdn_conv": nrm(ks[6], (DEPTH, DEC_BATCH, GDN_CONV - 1, GDN_CONV_DIM), 1.0),
        "state_ffn_conv": nrm(ks[7], (DEPTH, DEC_BATCH, FFN_CONV - 1, 2 * D_FF), 1.0),
        "w_in": nrm(ks[8], (DEPTH, D_MODEL, IN_COLS), D_MODEL ** -0.5),
        "gdn_conv_w": nrm(ks[9], (DEPTH, GDN_CONV, GDN_CONV_DIM), GDN_CONV ** -0.5),
        "gdn_A_log": jnp.log(jax.random.uniform(ks[10], (DEPTH, GDN_HEADS), f32, 1.0, 16.0)),
        "gdn_dt_bias": jax.random.uniform(ks[11], (DEPTH, GDN_HEADS), f32, -3.0, -1.0),
        "gdn_norm_w": gain(ks[12], GDN_DV),
        "mla_q_norm_w": gain(ks[13], MLA_Q_LORA),
        "mla_w_uq": nrm(ks[14], (DEPTH, MLA_Q_LORA, MLA_HEADS * (MLA_NOPE + MLA_ROPE)), MLA_Q_LORA ** -0.5),
        "mla_kv_norm_w": gain(ks[15], MLA_KV_LORA),
        "mla_w_uk": nrm(ks[16], (DEPTH, MLA_KV_LORA, MLA_HEADS, MLA_NOPE), MLA_KV_LORA ** -0.5),
        "mla_w_uv": nrm(ks[17], (DEPTH, MLA_KV_LORA, MLA_HEADS, MLA_V), MLA_KV_LORA ** -0.5),
        "w_out": nrm(ks[18], (DEPTH, MIX_WIDTH, D_MODEL), BETA_INIT * MIX_WIDTH ** -0.5),
        "ln1_g": gain(ks[19], D_MODEL),
        "ln1_b": nrm(ks[20], (DEPTH, D_MODEL), 0.01),
        "ffn_w_up": nrm(ks[21], (DEPTH, D_MODEL, 2 * D_FF), D_MODEL ** -0.5),
        "ffn_conv_w": nrm(ks[22], (DEPTH, FFN_CONV, 2 * D_FF), FFN_CONV ** -0.5),
        "ffn_w_down": nrm(ks[23], (DEPTH, D_FF, D_MODEL), BETA_INIT * D_FF ** -0.5),
        "ln2_g": gain(ks[24], D_MODEL),
        "ln2_b": nrm(ks[25], (DEPTH, D_MODEL), 0.01),
    }


def reference(x_prompt, x_sample, cache_ckv, cache_kpe, page_table, state_gdn, state_gdn_conv,
              state_ffn_conv, w_in, gdn_conv_w, gdn_A_log, gdn_dt_bias, gdn_norm_w,
              mla_q_norm_w, mla_w_uq, mla_kv_norm_w, mla_w_uk, mla_w_uv, w_out, ln1_g, ln1_b,
              ffn_w_up, ffn_conv_w, ffn_w_down, ln2_g, ln2_b):
    B = x_prompt.shape[0]
    DB = x_sample.shape[0]
    n_pages = page_table.shape[1]
    dt = x_prompt.dtype
    xp, xs = x_prompt, x_sample
    new_p = ([], [], [], [], [])
    new_s = ([], [], [], [], [])
    for l in range(DEPTH):
        params = (w_in[l], gdn_conv_w[l], gdn_A_log[l], gdn_dt_bias[l], gdn_norm_w[l],
                  mla_q_norm_w[l], mla_w_uq[l], mla_kv_norm_w[l], mla_w_uk[l], mla_w_uv[l],
                  w_out[l], ln1_g[l], ln1_b[l], ffn_w_up[l], ffn_conv_w[l], ffn_w_down[l],
                  ln2_g[l], ln2_b[l])
        xp, *st_p = hybrid_layer(
            xp, jnp.zeros((B, 0, MLA_KV_LORA), dt), jnp.zeros((B, 0, MLA_ROPE), dt),
            jnp.zeros((B, GDN_HEADS, GDN_DK, GDN_DV), dt),
            jnp.zeros((B, GDN_CONV - 1, GDN_CONV_DIM), dt),
            jnp.zeros((B, FFN_CONV - 1, 2 * D_FF), dt), *params)
        past_ckv = cache_ckv[l][page_table].reshape(DB, n_pages * PAGE_SIZE, MLA_KV_LORA)
        past_kpe = cache_kpe[l][page_table].reshape(DB, n_pages * PAGE_SIZE, MLA_ROPE)
        xs, *st_s = hybrid_layer(xs, past_ckv, past_kpe, state_gdn[l], state_gdn_conv[l],
                                 state_ffn_conv[l], *params)
        for lst, val in zip(new_p, st_p):
            lst.append(val)
        for lst, val in zip(new_s, st_s):
            lst.append(val)
    p_ckv, p_kpe, p_gdn, p_gdn_conv, p_ffn_conv = [jnp.stack(v) for v in new_p]
    s_ckv, s_kpe, s_gdn, s_gdn_conv, s_ffn_conv = [jnp.stack(v) for v in new_s]
    return (xp, xs, p_ckv, p_kpe, p_gdn, p_gdn_conv, p_ffn_conv,
            s_ckv, s_kpe, s_gdn, s_gdn_conv, s_ffn_conv)
```

```python
import functools
import math

import jax
import jax.numpy as jnp
from jax import lax
from jax.experimental import pallas as pl
from jax.experimental.pallas import tpu as pltpu

F32 = jnp.float32
BF16 = jnp.bfloat16

GDN_H = 8
GDN_D = 64
GDN_QK = GDN_H * GDN_D
GDN_QKV = 3 * GDN_QK
GDN_CONV_TAPS = 4
MLA_H = 8
MLA_Q_RANK = 256
MLA_KV_RANK = 128
MLA_NOPE_D = 64
MLA_ROPE_D = 32
MLA_V_D = 64
MLA_QK = MLA_KV_RANK + MLA_ROPE_D
MLA_SOFTMAX_SCALE = (MLA_NOPE_D + MLA_ROPE_D) ** -0.5
ROPE_BASE = 10000.0
FFN_CONV_TAPS = 3
RMS_EPS = 1e-6
LN_EPS = 1e-5
L2_EPS = 1e-6

SUBLANES = 8
LANES = 128
VMEM_LIMIT_BYTES = 56 * 1024 * 1024
FFN_CHUNK = 256

_NT = (((1,), (1,)), ((), ()))


def _mm(a, b):
    return jnp.dot(a.astype(BF16), b.astype(BF16), preferred_element_type=F32)


def _mm_nt(a, b):
    return lax.dot_general(a.astype(BF16), b.astype(BF16), _NT, preferred_element_type=F32)


def _sigmoid(x):
    return 1.0 / (1.0 + jnp.exp(-x))


def _silu(x):
    return x * _sigmoid(x)


def _rms(x, w):
    return x * lax.rsqrt(jnp.mean(x * x, axis=-1, keepdims=True) + RMS_EPS) * w


def _layernorm(x, g, b):
    mu = jnp.mean(x, axis=-1, keepdims=True)
    xc = x - mu
    var = jnp.mean(xc * xc, axis=-1, keepdims=True)
    return xc * lax.rsqrt(var + LN_EPS) * g + b


def _const_spec(shape):
    nd = len(shape)
    return pl.BlockSpec(shape, lambda *_: (0,) * nd)


def _in_proj_kernel(x_ref, w_qkv_ref, w_z_ref, w_rest_ref, w_q_ref, w_ukt_ref, qn_ref, kvn_ref,
                    cos_ref, sin_ref,
                    qkv_ref, z_ref, small_ref, ckv_ref, kpe_ref, q_ref, k_ref):
    xb = x_ref[...].astype(BF16)
    qkv_ref[...] = jnp.dot(xb, w_qkv_ref[...], preferred_element_type=F32)
    z_ref[...] = jnp.dot(xb, w_z_ref[...], preferred_element_type=F32)
    rest = jnp.dot(xb, w_rest_ref[...], preferred_element_type=F32)
    cq = rest[:, 0:MLA_Q_RANK]
    ckv_raw = rest[:, MLA_Q_RANK:MLA_Q_RANK + MLA_KV_RANK]
    small = rest[:, MLA_Q_RANK + MLA_KV_RANK:]
    small_ref[...] = small

    cos = cos_ref[...]
    sin = sin_ref[...]
    c_new = _rms(ckv_raw, kvn_ref[...])
    ckv_ref[...] = c_new
    kpe = small[:, 16:48] * cos[:, 0:MLA_ROPE_D] + small[:, 48:80] * sin[:, 0:MLA_ROPE_D]
    kpe_ref[...] = kpe
    k_ref[:, 0:MLA_KV_RANK] = c_new.astype(k_ref.dtype)
    k_ref[:, MLA_KV_RANK:MLA_QK] = kpe.astype(k_ref.dtype)

    cqn = _rms(cq, qn_ref[...])
    q = _mm(cqn, w_q_ref[...])
    n_nope = MLA_H * MLA_NOPE_D
    npe = MLA_H * MLA_ROPE_D
    q_pe = (q[:, n_nope:n_nope + npe] * cos + q[:, n_nope + npe:] * sin) * MLA_SOFTMAX_SCALE
    for h in range(MLA_H):
        q_lat = _mm(q[:, h * MLA_NOPE_D:(h + 1) * MLA_NOPE_D], w_ukt_ref[h]) * MLA_SOFTMAX_SCALE
        q_ref[h, :, 0:MLA_KV_RANK] = q_lat.astype(q_ref.dtype)
        q_ref[h, :, MLA_KV_RANK:MLA_QK] = q_pe[:, h * MLA_ROPE_D:(h + 1) * MLA_ROPE_D].astype(q_ref.dtype)


def _in_proj(x2, wts, cos_t, sin_t, tm, act_dtype):
    m, d = x2.shape
    n_rope_tiles = cos_t.shape[0] // tm
    row = lambda i: (i, 0)
    rope_row = lambda i: (i % n_rope_tiles, 0)
    out_shape = (
        jax.ShapeDtypeStruct((m, GDN_QKV), F32),
        jax.ShapeDtypeStruct((m, GDN_QK), F32),
        jax.ShapeDtypeStruct((m, LANES), F32),
        jax.ShapeDtypeStruct((m, MLA_KV_RANK), F32),
        jax.ShapeDtypeStruct((m, MLA_ROPE_D), F32),
        jax.ShapeDtypeStruct((MLA_H, m, MLA_QK), act_dtype),
        jax.ShapeDtypeStruct((m, MLA_QK), act_dtype),
    )
    return pl.pallas_call(
        _in_proj_kernel,
        grid=(m // tm,),
        in_specs=[
            pl.BlockSpec((tm, d), row),
            _const_spec(wts["w_qkv"].shape), _const_spec(wts["w_z"].shape),
            _const_spec(wts["w_rest"].shape), _const_spec(wts["w_q"].shape),
            _const_spec(wts["w_ukt"].shape), _const_spec(wts["q_norm"].shape),
            _const_spec(wts["kv_norm"].shape),
            pl.BlockSpec((tm, MLA_H * MLA_ROPE_D), rope_row),
            pl.BlockSpec((tm, MLA_H * MLA_ROPE_D), rope_row),
        ],
        out_specs=(
            pl.BlockSpec((tm, GDN_QKV), row), pl.BlockSpec((tm, GDN_QK), row),
            pl.BlockSpec((tm, LANES), row), pl.BlockSpec((tm, MLA_KV_RANK), row),
            pl.BlockSpec((tm, MLA_ROPE_D), row),
            pl.BlockSpec((MLA_H, tm, MLA_QK), lambda i: (0, i, 0)),
            pl.BlockSpec((tm, MLA_QK), row),
        ),
        out_shape=out_shape,
        compiler_params=pltpu.CompilerParams(
            dimension_semantics=("parallel",), vmem_limit_bytes=VMEM_LIMIT_BYTES),
        name="in_proj",
    )(x2, wts["w_qkv"], wts["w_z"], wts["w_rest"], wts["w_q"], wts["w_ukt"], wts["q_norm"],
      wts["kv_norm"], cos_t, sin_t)


def _gdn_kernel(qkv_ref, z_ref, small_ref, hist_ref, s0_ref, convw_ref, gpar_ref, normw_ref,
                o_ref, sout_ref, xbuf, s_scr, *, tt, chunk):
    t = pl.program_id(1)

    @pl.when(t == 0)
    def _():
        xbuf[0:SUBLANES, :] = hist_ref[0]
        s_scr[...] = s0_ref[0]

    xbuf[SUBLANES:SUBLANES + tt, :] = qkv_ref[0]
    cw = convw_ref[...]
    y = xbuf[pl.ds(SUBLANES - 3, tt), :] * cw[0:1]
    for j in range(1, GDN_CONV_TAPS):
        y = y + xbuf[pl.ds(SUBLANES - 3 + j, tt), :] * cw[j:j + 1]
    qkv = _silu(y)
    xbuf[0:SUBLANES, :] = xbuf[tt:tt + SUBLANES, :]

    small = small_ref[0]
    beta_all = _sigmoid(small)
    xa = small + gpar_ref[1:2, :]
    softplus = jnp.maximum(xa, 0.0) + jnp.log1p(jnp.exp(-jnp.abs(xa)))
    g_all = -jnp.exp(gpar_ref[0:1, :]) * softplus
    z = z_ref[0]
    norm_w = normw_ref[...]

    ri = lax.broadcasted_iota(jnp.int32, (chunk, chunk), 0)
    ci = lax.broadcasted_iota(jnp.int32, (chunk, chunk), 1)
    incl = ri >= ci
    strict = ri > ci
    tril = incl.astype(F32)
    eye_c = (ri == ci).astype(F32)
    li = lax.broadcasted_iota(jnp.int32, (LANES, LANES), 0)
    lj = lax.broadcasted_iota(jnp.int32, (LANES, LANES), 1)
    eye_l = (li == lj).astype(F32)
    di = lax.broadcasted_iota(jnp.int32, (GDN_D, GDN_D), 0)
    dj = lax.broadcasted_iota(jnp.int32, (GDN_D, GDN_D), 1)
    eye_d = (di == dj).astype(BF16)
    n_levels = int(math.log2(chunk)) - 1

    for c in range(tt // chunk):
        rows = slice(c * chunk, (c + 1) * chunk)
        gc = jnp.dot(tril, g_all[rows], precision=lax.Precision.HIGHEST, preferred_element_type=F32)
        gc_t = lax.dot_general(eye_l, gc, _NT, precision=lax.Precision.HIGHEST,
                               preferred_element_type=F32)
        outs = []
        for h in range(GDN_H):
            lane = SUBLANES + h
            qh = qkv[rows, h * GDN_D:(h + 1) * GDN_D]
            kh = qkv[rows, GDN_QK + h * GDN_D:GDN_QK + (h + 1) * GDN_D]
            vh = qkv[rows, 2 * GDN_QK + h * GDN_D:2 * GDN_QK + (h + 1) * GDN_D]
            qh = qh * lax.rsqrt(jnp.sum(qh * qh, axis=-1, keepdims=True) + L2_EPS) * (GDN_D ** -0.5)
            kh = kh * lax.rsqrt(jnp.sum(kh * kh, axis=-1, keepdims=True) + L2_EPS)
            bh = beta_all[rows, h:h + 1]
            g_col = gc[:, lane:lane + 1]
            g_row = gc_t[lane:lane + 1, :]
            g_last = gc[chunk - 1:chunk, lane:lane + 1]
            decay = jnp.exp(jnp.where(incl, g_col - g_row, -jnp.inf))
            e_col = jnp.exp(g_col)
            kb = kh * bh
            kh_b = kh.astype(BF16)
            lmat = jnp.where(strict, _mm_nt(kb, kh_b) * decay, 0.0)
            inv = eye_c - lmat
            pw = lmat
            for _ in range(n_levels):
                pw = _mm(pw, pw)
                inv = inv + _mm(inv, pw)
            inv_b = inv.astype(BF16)
            u = _mm(inv_b, vh * bh)
            w = _mm(inv_b, kb * e_col)
            a = _mm_nt(qh, kh_b) * decay
            s_h = s_scr[h]
            s_b = s_h.astype(BF16)
            v_new = u - _mm(w, s_b)
            v_new_b = v_new.astype(BF16)
            o = _mm(qh * e_col, s_b) + _mm(a, v_new_b)
            kg = kh * jnp.exp(g_last - g_col)
            kg_t = lax.dot_general(eye_d, kg.astype(BF16), _NT, preferred_element_type=F32)
            s_scr[h] = s_h * jnp.exp(g_last) + _mm(kg_t, v_new_b)
            zh = z[rows, h * GDN_D:(h + 1) * GDN_D]
            outs.append(_rms(o, norm_w) * _silu(zh))
        o_ref[0, rows, :] = jnp.concatenate(outs, axis=-1).astype(o_ref.dtype)

    @pl.when(t == pl.num_programs(1) - 1)
    def _():
        sout_ref[0] = s_scr[...]


def _gdn(qkv, z, small, hist8, s0, conv_w, gpar, norm_w, tt, chunk, act_dtype):
    b, t, _ = qkv.shape
    seq = lambda i, j: (i, j, 0)
    per_b3 = lambda i, j: (i, 0, 0)
    per_b4 = lambda i, j: (i, 0, 0, 0)
    return pl.pallas_call(
        functools.partial(_gdn_kernel, tt=tt, chunk=chunk),
        grid=(b, t // tt),
        in_specs=[
            pl.BlockSpec((1, tt, GDN_QKV), seq), pl.BlockSpec((1, tt, GDN_QK), seq),
            pl.BlockSpec((1, tt, LANES), seq),
            pl.BlockSpec((1, SUBLANES, GDN_QKV), per_b3),
            pl.BlockSpec((1, GDN_H, GDN_D, GDN_D), per_b4),
            _const_spec(conv_w.shape), _const_spec(gpar.shape), _const_spec(norm_w.shape),
        ],
        out_specs=(pl.BlockSpec((1, tt, GDN_QK), seq),
                   pl.BlockSpec((1, GDN_H, GDN_D, GDN_D), per_b4)),
        out_shape=(jax.ShapeDtypeStruct((b, t, GDN_QK), act_dtype),
                   jax.ShapeDtypeStruct((b, GDN_H, GDN_D, GDN_D), F32)),
        scratch_shapes=[pltpu.VMEM((tt + SUBLANES, GDN_QKV), F32),
                        pltpu.VMEM((GDN_H, GDN_D, GDN_D), F32)],
        compiler_params=pltpu.CompilerParams(
            dimension_semantics=("parallel", "arbitrary"), vmem_limit_bytes=VMEM_LIMIT_BYTES),
        name="gdn",
    )(qkv, z, small, hist8, s0, conv_w, gpar, norm_w)


def _softmax_update(s, v_b, m_scr, l_scr, acc_scr):
    m_prev = m_scr[...]
    m_new = jnp.maximum(m_prev, jnp.max(s, axis=-1, keepdims=True))
    alpha = jnp.exp(m_prev - m_new)
    p = jnp.exp(s - m_new)
    l_scr[...] = alpha * l_scr[...] + jnp.sum(p, axis=-1, keepdims=True)
    acc_scr[...] = alpha * acc_scr[...] + jnp.dot(p.astype(BF16), v_b, preferred_element_type=F32)
    m_scr[...] = m_new


def _attn_finalize(w_uv_ref, o_ref, l_scr, acc_scr, tq):
    o_lat = acc_scr[...] / l_scr[...]
    outs = [_mm(o_lat[h * tq:(h + 1) * tq], w_uv_ref[h]) for h in range(MLA_H)]
    o_ref[...] = jnp.concatenate(outs, axis=-1).astype(o_ref.dtype)


def _attn_prompt_kernel(q_ref, k_ref, w_uv_ref, o_ref, m_scr, l_scr, acc_scr, *, tq):
    i = pl.program_id(1)
    j = pl.program_id(2)
    rows = MLA_H * tq

    @pl.when(j == 0)
    def _():
        m_scr[...] = jnp.full(m_scr.shape, -jnp.inf, F32)
        l_scr[...] = jnp.zeros(l_scr.shape, F32)
        acc_scr[...] = jnp.zeros(acc_scr.shape, F32)

    def update(masked):
        q = q_ref[...].reshape(rows, MLA_QK)
        k = k_ref[...]
        s = lax.dot_general(q, k, _NT, preferred_element_type=F32)
        if masked:
            qi = lax.broadcasted_iota(jnp.int32, (MLA_H, tq, tq), 1)
            ki = lax.broadcasted_iota(jnp.int32, (MLA_H, tq, tq), 2)
            s = jnp.where(ki <= qi, s.reshape(MLA_H, tq, tq), -jnp.inf).reshape(rows, tq)
        _softmax_update(s, k[:, 0:MLA_KV_RANK], m_scr, l_scr, acc_scr)

    @pl.when(j < i)
    def _():
        update(False)

    @pl.when(j == i)
    def _():
        update(True)
        _attn_finalize(w_uv_ref, o_ref, l_scr, acc_scr, tq)


def _attn_prompt(q, k, w_uv, b, t, tq):
    nq = t // tq
    rows = MLA_H * tq
    m = b * t
    return pl.pallas_call(
        functools.partial(_attn_prompt_kernel, tq=tq),
        grid=(b, nq, nq),
        in_specs=[
            pl.BlockSpec((MLA_H, tq, MLA_QK), lambda bi, i, j: (0, bi * nq + i, 0)),
            pl.BlockSpec((tq, MLA_QK), lambda bi, i, j: (bi * nq + jnp.minimum(i, j), 0)),
            _const_spec(w_uv.shape),
        ],
        out_specs=pl.BlockSpec((tq, MLA_H * MLA_V_D), lambda bi, i, j: (bi * nq + i, 0)),
        out_shape=jax.ShapeDtypeStruct((m, MLA_H * MLA_V_D), BF16),
        scratch_shapes=[pltpu.VMEM((rows, 1), F32), pltpu.VMEM((rows, 1), F32),
                        pltpu.VMEM((rows, MLA_KV_RANK), F32)],
        compiler_params=pltpu.CompilerParams(
            dimension_semantics=("parallel", "parallel", "arbitrary"),
            vmem_limit_bytes=VMEM_LIMIT_BYTES),
        name="attn_prompt",
    )(q, k, w_uv)


def _attn_sample_kernel(pt_ref, q_ref, knew_ref, *refs, pages_per_step, t_new):
    del pt_ref
    ckv_refs = refs[:pages_per_step]
    kpe_refs = refs[pages_per_step:2 * pages_per_step]
    w_uv_ref, o_ref, m_scr, l_scr, acc_scr = refs[2 * pages_per_step:]
    step = pl.program_id(1)
    rows = MLA_H * t_new

    @pl.when(step == 0)
    def _():
        m_scr[...] = jnp.full(m_scr.shape, -jnp.inf, F32)
        l_scr[...] = jnp.zeros(l_scr.shape, F32)
        acc_scr[...] = jnp.zeros(acc_scr.shape, F32)

    q = q_ref[...].reshape(rows, MLA_QK)
    q_lat = q[:, 0:MLA_KV_RANK].astype(BF16)
    q_pe = q[:, MLA_KV_RANK:MLA_QK].astype(BF16)
    for p in range(pages_per_step):
        c_b = ckv_refs[p][...].astype(BF16)
        s = _mm_nt(q_lat, c_b) + _mm_nt(q_pe, kpe_refs[p][...])
        _softmax_update(s, c_b, m_scr, l_scr, acc_scr)

    @pl.when(step == pl.num_programs(1) - 1)
    def _():
        k_new = knew_ref[...]
        s = _mm_nt(q, k_new)
        qi = lax.broadcasted_iota(jnp.int32, (rows, t_new), 0) % t_new
        ki = lax.broadcasted_iota(jnp.int32, (rows, t_new), 1)
        s = jnp.where(ki <= qi, s, -jnp.inf)
        _softmax_update(s, k_new[:, 0:MLA_KV_RANK].astype(BF16), m_scr, l_scr, acc_scr)
        _attn_finalize(w_uv_ref, o_ref, l_scr, acc_scr, t_new)


def _attn_sample(q, k_new, cache_ckv, cache_kpe, page_table, w_uv, b, t_new, pages_per_step):
    n_pages = page_table.shape[1]
    page = cache_ckv.shape[1]
    rows = MLA_H * t_new
    n_steps = n_pages // pages_per_step

    def page_map(p):
        return lambda bi, s, pt: (pt[bi, s * pages_per_step + p], 0, 0)

    grid_spec = pltpu.PrefetchScalarGridSpec(
        num_scalar_prefetch=1,
        grid=(b, n_steps),
        in_specs=(
            [pl.BlockSpec((MLA_H, t_new, MLA_QK), lambda bi, s, pt: (0, bi, 0)),
             pl.BlockSpec((t_new, MLA_QK), lambda bi, s, pt: (bi, 0))]
            + [pl.BlockSpec((None, page, MLA_KV_RANK), page_map(p)) for p in range(pages_per_step)]
            + [pl.BlockSpec((None, page, MLA_ROPE_D), page_map(p)) for p in range(pages_per_step)]
            + [pl.BlockSpec(w_uv.shape, lambda bi, s, pt: (0, 0, 0))]
        ),
        out_specs=pl.BlockSpec((t_new, MLA_H * MLA_V_D), lambda bi, s, pt: (bi, 0)),
        scratch_shapes=[pltpu.VMEM((rows, 1), F32), pltpu.VMEM((rows, 1), F32),
                        pltpu.VMEM((rows, MLA_KV_RANK), F32)],
    )
    return pl.pallas_call(
        functools.partial(_attn_sample_kernel, pages_per_step=pages_per_step, t_new=t_new),
        grid_spec=grid_spec,
        out_shape=jax.ShapeDtypeStruct((b * t_new, MLA_H * MLA_V_D), F32),
        compiler_params=pltpu.CompilerParams(
            dimension_semantics=("parallel", "arbitrary"), vmem_limit_bytes=VMEM_LIMIT_BYTES),
        name="attn_sample",
    )(page_table, q, k_new, *([cache_ckv] * pages_per_step), *([cache_kpe] * pages_per_step), w_uv)


def _ffn_kernel(x_ref, og_ref, om_ref, hist_ref, w_out_ref, ln_ref, w_up_ref, cw_ref, w_dn_ref,
                y_ref, nf_ref, *scratch, tm, alpha, short_seq):
    d_gdn = og_ref.shape[-1]
    n_chunks, _, two_fc = w_up_ref.shape
    fc = two_fc // 2
    mix = _mm(og_ref[...], w_out_ref[0:d_gdn, :]) + _mm(om_ref[...], w_out_ref[d_gdn:, :])
    x1 = _layernorm(alpha * x_ref[...] + mix, ln_ref[0:1, :], ln_ref[1:2, :])
    x1_b = x1.astype(BF16)

    if short_seq:
        nb = tm // SUBLANES
        t_pos = lax.broadcasted_iota(jnp.int32, (nb, SUBLANES, two_fc), 1)
    else:
        ubuf, carry = scratch

        @pl.when(pl.program_id(1) == 0)
        def _():
            carry[...] = hist_ref[0]

    acc = jnp.zeros((tm, x_ref.shape[-1]), F32)
    for c in range(n_chunks):
        cols = slice(c * two_fc, (c + 1) * two_fc)
        up = jnp.dot(x1_b, w_up_ref[c], preferred_element_type=F32)
        cw = cw_ref[c]
        if short_seq:
            up3 = up.reshape(nb, SUBLANES, two_fc)
            h0 = hist_ref[:, SUBLANES - 2:SUBLANES - 1, cols]
            h1 = hist_ref[:, SUBLANES - 1:SUBLANES, cols]
            prev1 = jnp.where(t_pos == 0, h1, pltpu.roll(up3, 1, axis=1))
            prev2 = jnp.where(t_pos == 0, h0, jnp.where(t_pos == 1, h1, pltpu.roll(up3, 2, axis=1)))
            hc = (prev2 * cw[0:1] + prev1 * cw[1:2] + up3 * cw[2:3]).reshape(tm, two_fc)
            nf_ref[:, :, cols] = up3
        else:
            ubuf[0:SUBLANES, :] = carry[:, cols]
            ubuf[SUBLANES:SUBLANES + tm, :] = up
            hc = (ubuf[pl.ds(SUBLANES - 2, tm), :] * cw[0:1]
                  + ubuf[pl.ds(SUBLANES - 1, tm), :] * cw[1:2] + up * cw[2:3])
            last = ubuf[tm:tm + SUBLANES, :]
            carry[:, cols] = last
            nf_ref[0, :, cols] = last
        hmid = _silu(hc[:, fc:]) * hc[:, 0:fc]
        acc = acc + _mm(hmid, w_dn_ref[c])
    y_ref[...] = _layernorm(alpha * x1 + acc, ln_ref[2:3, :], ln_ref[3:4, :])


def _ffn(x2, o_gdn, o_mla, hist8, wts, b, t, tm, alpha):
    m, d = x2.shape
    n_chunks, _, two_fc = wts["w_up"].shape
    two_ff = n_chunks * two_fc
    short_seq = t == SUBLANES
    if short_seq:
        grid = (m // tm, 1)
        nb = tm // SUBLANES
        row = lambda i, j: (i, 0)
        hist_spec = pl.BlockSpec((nb, SUBLANES, two_ff), lambda i, j: (i, 0, 0))
        scratch = []
    else:
        nt = t // tm
        grid = (b, nt)
        row = lambda i, j: (i * nt + j, 0)
        hist_spec = pl.BlockSpec((1, SUBLANES, two_ff), lambda i, j: (i, 0, 0))
        scratch = [pltpu.VMEM((tm + SUBLANES, two_fc), F32), pltpu.VMEM((SUBLANES, two_ff), F32)]
    return pl.pallas_call(
        functools.partial(_ffn_kernel, tm=tm, alpha=alpha, short_seq=short_seq),
        grid=grid,
        in_specs=[
            pl.BlockSpec((tm, d), row), pl.BlockSpec((tm, o_gdn.shape[1]), row),
            pl.BlockSpec((tm, o_mla.shape[1]), row), hist_spec,
            _const_spec(wts["w_out"].shape), _const_spec(wts["ln"].shape),
            _const_spec(wts["w_up"].shape), _const_spec(wts["ffn_cw"].shape),
            _const_spec(wts["w_dn"].shape),
        ],
        out_specs=(pl.BlockSpec((tm, d), row), hist_spec),
        out_shape=(jax.ShapeDtypeStruct((m, d), F32),
                   jax.ShapeDtypeStruct((b, SUBLANES, two_ff), F32)),
        scratch_shapes=scratch,
        compiler_params=pltpu.CompilerParams(
            dimension_semantics=("parallel", "arbitrary"), vmem_limit_bytes=VMEM_LIMIT_BYTES),
        name="ffn",
    )(x2, o_gdn, o_mla, hist8, wts["w_out"], wts["ln"], wts["w_up"], wts["ffn_cw"], wts["w_dn"])


def _swap_halves(w):
    half = w.shape[-1] // 2
    return jnp.concatenate([-w[..., half:], w[..., :half]], axis=-1)


def _prep_weights(w_in, gdn_conv_w, gdn_A_log, gdn_dt_bias, gdn_norm_w, mla_q_norm_w, mla_w_uq,
                  mla_kv_norm_w, mla_w_uk, mla_w_uv, w_out, ln1_g, ln1_b, ffn_w_up, ffn_conv_w,
                  ffn_w_down, ln2_g, ln2_b):
    d_model = w_in.shape[0]
    o = 0
    w_qkv = w_in[:, o:o + GDN_QKV]; o += GDN_QKV
    w_z = w_in[:, o:o + GDN_QK]; o += GDN_QK
    w_b = w_in[:, o:o + GDN_H]; o += GDN_H
    w_a = w_in[:, o:o + GDN_H]; o += GDN_H
    w_cq = w_in[:, o:o + MLA_Q_RANK]; o += MLA_Q_RANK
    w_ckv = w_in[:, o:o + MLA_KV_RANK]; o += MLA_KV_RANK
    w_kpe = w_in[:, o:o + MLA_ROPE_D]
    pad = jnp.zeros((d_model, LANES - 2 * GDN_H - 2 * MLA_ROPE_D), w_in.dtype)
    w_small = jnp.concatenate([w_b, w_a, w_kpe, _swap_halves(w_kpe), pad], axis=1)
    w_rest = jnp.concatenate([w_cq, w_ckv, w_small], axis=1)

    uq = mla_w_uq.reshape(MLA_Q_RANK, MLA_H, MLA_NOPE_D + MLA_ROPE_D)
    uq_nope = uq[:, :, :MLA_NOPE_D].reshape(MLA_Q_RANK, MLA_H * MLA_NOPE_D)
    uq_pe = uq[:, :, MLA_NOPE_D:]
    w_q = jnp.concatenate([uq_nope, uq_pe.reshape(MLA_Q_RANK, -1),
                           _swap_halves(uq_pe).reshape(MLA_Q_RANK, -1)], axis=1)

    gpar = jnp.zeros((2, LANES), F32)
    gpar = gpar.at[0, GDN_H:2 * GDN_H].set(gdn_A_log).at[1, GDN_H:2 * GDN_H].set(gdn_dt_bias)

    d_ff = ffn_w_down.shape[0]
    n_chunks = d_ff // FFN_CHUNK

    def interleave(w):
        lead = w.shape[:-1]
        w = w.reshape(lead + (2, n_chunks, FFN_CHUNK))
        return jnp.moveaxis(w, -3, -2).reshape(lead + (n_chunks, 2 * FFN_CHUNK))

    return {
        "w_qkv": w_qkv.astype(BF16), "w_z": w_z.astype(BF16), "w_rest": w_rest.astype(BF16),
        "w_q": w_q.astype(BF16),
        "w_ukt": jnp.transpose(mla_w_uk, (1, 2, 0)).astype(BF16),
        "w_uv": jnp.transpose(mla_w_uv, (1, 0, 2)).astype(BF16),
        "q_norm": mla_q_norm_w.reshape(1, -1), "kv_norm": mla_kv_norm_w.reshape(1, -1),
        "gdn_conv_w": gdn_conv_w, "gpar": gpar, "gdn_norm_w": gdn_norm_w.reshape(1, -1),
        "w_out": w_out.astype(BF16),
        "ln": jnp.stack([ln1_g, ln1_b, ln2_g, ln2_b]),
        "w_up": jnp.transpose(interleave(ffn_w_up), (1, 0, 2)).astype(BF16),
        "ffn_cw": jnp.transpose(interleave(ffn_conv_w), (1, 0, 2)),
        "w_dn": ffn_w_down.reshape(n_chunks, FFN_CHUNK, -1).astype(BF16),
        "interleave": interleave, "n_chunks": n_chunks,
    }


def _rope_tables(past_len, t, tm):
    half = MLA_ROPE_D // 2
    pos = (past_len + jnp.arange(t, dtype=jnp.int32)).astype(F32)
    inv = ROPE_BASE ** (-jnp.arange(half, dtype=F32) / half)
    ang = pos[:, None] * inv
    reps = (max(tm // t, 1), 2 * MLA_H)
    return jnp.tile(jnp.cos(ang), reps), jnp.tile(jnp.sin(ang), reps)


def _pad_hist(h):
    return jnp.pad(h, ((0, 0), (SUBLANES - h.shape[1], 0), (0, 0)))


def _layer(x, past, s0, conv_hist, ffn_hist, wts, alpha, *, tm_proj, tm_ffn, tq, gdn_tt, gdn_chunk):
    b, t, d = x.shape
    m = b * t
    x2 = x.reshape(m, d)
    short_seq = past is not None
    past_len = past[2].shape[1] * past[0].shape[1] if short_seq else 0
    act_dtype = F32 if short_seq else BF16
    cos_t, sin_t = _rope_tables(past_len, t, tm_proj)
    qkv, z, small, c_new, kpe_new, q, k = _in_proj(x2, wts, cos_t, sin_t, tm_proj, act_dtype)

    o_gdn, s_new = _gdn(qkv.reshape(b, t, -1), z.reshape(b, t, -1), small.reshape(b, t, -1),
                        _pad_hist(conv_hist), s0, wts["gdn_conv_w"], wts["gpar"], wts["gdn_norm_w"],
                        gdn_tt, gdn_chunk, act_dtype)
    if short_seq:
        cache_ckv, cache_kpe, page_table = past
        o_mla = _attn_sample(q, k, cache_ckv, cache_kpe, page_table, wts["w_uv"], b, t,
                             math.gcd(8, page_table.shape[1]))
    else:
        o_mla = _attn_prompt(q, k, wts["w_uv"], b, t, tq)

    hist8 = wts["interleave"](_pad_hist(ffn_hist)).reshape(b, SUBLANES, -1)
    y, nf8 = _ffn(x2, o_gdn.reshape(m, -1), o_mla, hist8, wts, b, t, tm_ffn, alpha)

    n_chunks = wts["n_chunks"]
    nf = nf8[:, SUBLANES - (FFN_CONV_TAPS - 1):, :].reshape(b, FFN_CONV_TAPS - 1, n_chunks, 2, FFN_CHUNK)
    new_ffn = jnp.moveaxis(nf, -2, -3).reshape(b, FFN_CONV_TAPS - 1, -1)
    new_conv = qkv.reshape(b, t, -1)[:, t - (GDN_CONV_TAPS - 1):, :]
    return (y.reshape(b, t, d), c_new.reshape(b, t, -1), kpe_new.reshape(b, t, -1), s_new,
            new_conv, new_ffn)


def kernel(x_prompt, x_sample, cache_ckv, cache_kpe, page_table, state_gdn, state_gdn_conv,
           state_ffn_conv, w_in, gdn_conv_w, gdn_A_log, gdn_dt_bias, gdn_norm_w, mla_q_norm_w,
           mla_w_uq, mla_kv_norm_w, mla_w_uk, mla_w_uv, w_out, ln1_g, ln1_b, ffn_w_up, ffn_conv_w,
           ffn_w_down, ln2_g, ln2_b):
    depth = w_in.shape[0]
    alpha = (2.0 * depth) ** 0.25
    bp, tp, _ = x_prompt.shape
    bs, ts, _ = x_sample.shape
    assert ts == SUBLANES, "sample group: one 8-row tile per sequence"
    xp, xs = x_prompt, x_sample
    new_p, new_s = [], []
    for l in range(depth):
        wts = _prep_weights(w_in[l], gdn_conv_w[l], gdn_A_log[l], gdn_dt_bias[l], gdn_norm_w[l],
                            mla_q_norm_w[l], mla_w_uq[l], mla_kv_norm_w[l], mla_w_uk[l],
                            mla_w_uv[l], w_out[l], ln1_g[l], ln1_b[l], ffn_w_up[l], ffn_conv_w[l],
                            ffn_w_down[l], ln2_g[l], ln2_b[l])
        two_ff = ffn_w_up.shape[-1]
        xp, *st_p = _layer(
            xp, None, jnp.zeros((bp, GDN_H, GDN_D, GDN_D), F32),
            jnp.zeros((bp, GDN_CONV_TAPS - 1, GDN_QKV), F32),
            jnp.zeros((bp, FFN_CONV_TAPS - 1, two_ff), F32), wts, alpha,
            tm_proj=min(256, tp), tm_ffn=min(256, tp), tq=min(256, tp), gdn_tt=64, gdn_chunk=64)
        xs, *st_s = _layer(
            xs, (cache_ckv[l], cache_kpe[l], page_table), state_gdn[l], state_gdn_conv[l],
            state_ffn_conv[l], wts, alpha,
            tm_proj=min(256, bs * ts), tm_ffn=min(256, bs * ts), tq=None, gdn_tt=ts, gdn_chunk=ts)
        new_p.append(st_p)
        new_s.append(st_s)
    stack = lambda sts: [jnp.stack(v) for v in zip(*sts)]
    return (xp, xs, *stack(new_p), *stack(new_s))
```

```python
import functools
import math

import jax
import jax.numpy as jnp
import numpy as np
from jax import lax
from jax.experimental import pallas as pl
from jax.experimental.pallas import tpu as pltpu

F32 = jnp.float32
BF16 = jnp.bfloat16

GDN_H = 8
GDN_D = 64
GDN_QK = GDN_H * GDN_D
GDN_QKV = 3 * GDN_QK
GDN_CONV_TAPS = 4
GDN_CHUNK = 64
GDN_GROUP = 256
GDN_GROUPS = GDN_QK // GDN_GROUP
HEADS_PER_GROUP = GDN_GROUP // GDN_D
MLA_H = 8
MLA_Q_RANK = 256
MLA_KV_RANK = 128
MLA_NOPE_D = 64
MLA_ROPE_D = 32
MLA_V_D = 64
MLA_QK = MLA_KV_RANK + MLA_ROPE_D
MLA_SOFTMAX_SCALE = (MLA_NOPE_D + MLA_ROPE_D) ** -0.5
ROPE_BASE = 10000.0
FFN_CONV_TAPS = 3
RMS_EPS = 1e-6
LN_EPS = 1e-5
L2_EPS = 1e-6

SUBLANES = 8
LANES = 128
VMEM_LIMIT_BYTES = 56 * 1024 * 1024
FFN_CHUNK = 256
SAMPLE_PAGES_PER_STEP = 16

_NT = (((1,), (1,)), ((), ()))
_TN = (((0,), (0,)), ((), ()))


def _mm(a, b):
    return jnp.dot(a.astype(BF16), b.astype(BF16), preferred_element_type=F32)


def _mm_nt(a, b):
    return lax.dot_general(a.astype(BF16), b.astype(BF16), _NT, preferred_element_type=F32)


def _sigmoid(x):
    return 1.0 / (1.0 + jnp.exp(-x))


def _silu(x):
    return x * _sigmoid(x)


def _rms(x, w):
    return x * lax.rsqrt(jnp.mean(x * x, axis=-1, keepdims=True) + RMS_EPS) * w


def _layernorm(x, g, b):
    mu = jnp.mean(x, axis=-1, keepdims=True)
    xc = x - mu
    var = jnp.mean(xc * xc, axis=-1, keepdims=True)
    return xc * lax.rsqrt(var + LN_EPS) * g + b


def _eye(n, dtype):
    return (lax.broadcasted_iota(jnp.int32, (n, n), 0)
            == lax.broadcasted_iota(jnp.int32, (n, n), 1)).astype(dtype)


def _const_spec(shape):
    nd = len(shape)
    return pl.BlockSpec(shape, lambda *_: (0,) * nd)


def _in_proj_kernel(x_ref, w_qkv_ref, w_z_ref, w_rest_ref, w_q_ref, w_uk_ref, qn_ref, kvn_ref,
                    *refs, transposed_q):
    if transposed_q:
        (cos_ref, sin_ref, cos_k_ref, sin_k_ref,
         qkv_ref, z_ref, small_ref, ckv_ref, kpe_ref, q_ref, k_ref, ct_ref) = refs
    else:
        cos_ref, sin_ref, qkv_ref, z_ref, small_ref, ckv_ref, kpe_ref, q_ref, k_ref = refs
    xb = x_ref[...].astype(BF16)
    qkv_ref[...] = jnp.dot(xb, w_qkv_ref[...], preferred_element_type=F32)
    z_ref[...] = jnp.dot(xb, w_z_ref[...], preferred_element_type=F32)
    rest = jnp.dot(xb, w_rest_ref[...], preferred_element_type=F32)
    cq = rest[:, 0:MLA_Q_RANK]
    ckv_raw = rest[:, MLA_Q_RANK:MLA_Q_RANK + MLA_KV_RANK]
    small = rest[:, MLA_Q_RANK + MLA_KV_RANK:]
    small_ref[...] = small

    c_new = _rms(ckv_raw, kvn_ref[...])
    ckv_ref[...] = c_new
    cqn = _rms(cq, qn_ref[...])
    n_nope = MLA_H * MLA_NOPE_D
    npe = MLA_H * MLA_ROPE_D
    k_rot_a = small[:, 16:16 + MLA_ROPE_D]
    k_rot_b = small[:, 16 + MLA_ROPE_D:16 + 2 * MLA_ROPE_D]

    if transposed_q:
        cos_t = cos_ref[...]
        sin_t = sin_ref[...]
        kpe = k_rot_a * cos_k_ref[...] + k_rot_b * sin_k_ref[...]
        ct_ref[...] = lax.dot_general(_eye(LANES, BF16), c_new.astype(BF16), _NT,
                                      preferred_element_type=F32).astype(ct_ref.dtype)
        q_t = lax.dot_general(w_q_ref[...], cqn.astype(BF16), _NT,
                              preferred_element_type=F32)
        q_pe = (q_t[n_nope:n_nope + npe] * cos_t + q_t[n_nope + npe:] * sin_t) * MLA_SOFTMAX_SCALE
        for h in range(MLA_H):
            q_lat = _mm(w_uk_ref[h], q_t[h * MLA_NOPE_D:(h + 1) * MLA_NOPE_D]) * MLA_SOFTMAX_SCALE
            q_ref[h, 0:MLA_KV_RANK, :] = q_lat.astype(q_ref.dtype)
            q_ref[h, MLA_KV_RANK:MLA_QK, :] = q_pe[h * MLA_ROPE_D:(h + 1) * MLA_ROPE_D].astype(q_ref.dtype)
    else:
        cos = cos_ref[...]
        sin = sin_ref[...]
        kpe = k_rot_a * cos[:, 0:MLA_ROPE_D] + k_rot_b * sin[:, 0:MLA_ROPE_D]
        q = _mm(cqn, w_q_ref[...])
        q_pe = (q[:, n_nope:n_nope + npe] * cos + q[:, n_nope + npe:] * sin) * MLA_SOFTMAX_SCALE
        for h in range(MLA_H):
            q_lat = _mm(q[:, h * MLA_NOPE_D:(h + 1) * MLA_NOPE_D], w_uk_ref[h]) * MLA_SOFTMAX_SCALE
            q_ref[h, :, 0:MLA_KV_RANK] = q_lat.astype(q_ref.dtype)
            q_ref[h, :, MLA_KV_RANK:MLA_QK] = q_pe[:, h * MLA_ROPE_D:(h + 1) * MLA_ROPE_D].astype(q_ref.dtype)

    kpe_ref[...] = kpe
    k_ref[:, 0:MLA_KV_RANK] = c_new.astype(k_ref.dtype)
    k_ref[:, MLA_KV_RANK:MLA_QK] = kpe.astype(k_ref.dtype)


def _in_proj(x2, wts, rope, tm, transposed_q):
    m, d = x2.shape
    row = lambda i: (i, 0)
    npe = MLA_H * MLA_ROPE_D
    act_dtype = BF16 if transposed_q else F32
    out_shape = [
        jax.ShapeDtypeStruct((m, GDN_QKV), F32),
        jax.ShapeDtypeStruct((m, GDN_QK), F32),
        jax.ShapeDtypeStruct((m, LANES), F32),
        jax.ShapeDtypeStruct((m, MLA_KV_RANK), F32),
        jax.ShapeDtypeStruct((m, MLA_ROPE_D), F32),
    ]
    out_specs = [
        pl.BlockSpec((tm, GDN_QKV), row), pl.BlockSpec((tm, GDN_QK), row),
        pl.BlockSpec((tm, LANES), row), pl.BlockSpec((tm, MLA_KV_RANK), row),
        pl.BlockSpec((tm, MLA_ROPE_D), row),
    ]
    if transposed_q:
        n_rope_tiles = rope[0].shape[1] // tm
        rope_specs = [pl.BlockSpec((npe, tm), lambda i: (0, i % n_rope_tiles))] * 2
        rope_specs += [pl.BlockSpec((tm, MLA_ROPE_D), lambda i: (i % n_rope_tiles, 0))] * 2
        w_q, w_uk = wts["w_q_t"], wts["w_uk"]
        out_shape += [jax.ShapeDtypeStruct((MLA_H, MLA_QK, m), act_dtype),
                      jax.ShapeDtypeStruct((m, MLA_QK), act_dtype),
                      jax.ShapeDtypeStruct((MLA_KV_RANK, m), act_dtype)]
        out_specs += [pl.BlockSpec((MLA_H, MLA_QK, tm), lambda i: (0, 0, i)),
                      pl.BlockSpec((tm, MLA_QK), row),
                      pl.BlockSpec((MLA_KV_RANK, tm), lambda i: (0, i))]
    else:
        n_rope_tiles = rope[0].shape[0] // tm
        rope_specs = [pl.BlockSpec((tm, npe), lambda i: (i % n_rope_tiles, 0))] * 2
        w_q, w_uk = wts["w_q"], wts["w_uk_t"]
        out_shape += [jax.ShapeDtypeStruct((MLA_H, m, MLA_QK), act_dtype),
                      jax.ShapeDtypeStruct((m, MLA_QK), act_dtype)]
        out_specs += [pl.BlockSpec((MLA_H, tm, MLA_QK), lambda i: (0, i, 0)),
                      pl.BlockSpec((tm, MLA_QK), row)]
    return pl.pallas_call(
        functools.partial(_in_proj_kernel, transposed_q=transposed_q),
        grid=(m // tm,),
        in_specs=[
            pl.BlockSpec((tm, d), row),
            _const_spec(wts["w_qkv"].shape), _const_spec(wts["w_z"].shape),
            _const_spec(wts["w_rest"].shape), _const_spec(w_q.shape), _const_spec(w_uk.shape),
            _const_spec(wts["q_norm"].shape), _const_spec(wts["kv_norm"].shape),
            *rope_specs,
        ],
        out_specs=tuple(out_specs),
        out_shape=tuple(out_shape),
        compiler_params=pltpu.CompilerParams(
            dimension_semantics=("parallel",), vmem_limit_bytes=VMEM_LIMIT_BYTES),
        name="in_proj",
    )(x2, wts["w_qkv"], wts["w_z"], wts["w_rest"], w_q, w_uk, wts["q_norm"], wts["kv_norm"],
      *rope)


def _gdn_kernel(qkv_ref, z_ref, small_ref, hist_ref, s0_ref, convw_ref, gpar_ref, normw_ref,
                o_ref, sout_ref, xbuf, s_scr, o_buf, *, t_in, tt):
    t = pl.program_id(1)
    c_rows, gw = GDN_CHUNK, GDN_GROUP
    padded = tt > t_in

    @pl.when(t == 0)
    def _():
        xbuf[0:SUBLANES, :] = hist_ref[0]
        s_scr[...] = jnp.zeros(s_scr.shape, F32)
        for h in range(GDN_H):
            g, hh = divmod(h, HEADS_PER_GROUP)
            s_scr[g, hh * GDN_D:(hh + 1) * GDN_D, hh * GDN_D:(hh + 1) * GDN_D] = s0_ref[0, h]

    if padded:
        xbuf[SUBLANES + t_in:SUBLANES + tt, :] = jnp.zeros((tt - t_in, GDN_QKV), F32)
    xbuf[SUBLANES:SUBLANES + t_in, :] = qkv_ref[0]
    cw = convw_ref[...]
    y = xbuf[pl.ds(SUBLANES - 3, tt), :] * cw[0:1]
    for j in range(1, GDN_CONV_TAPS):
        y = y + xbuf[pl.ds(SUBLANES - 3 + j, tt), :] * cw[j:j + 1]
    qkv = _silu(y)
    xbuf[0:SUBLANES, :] = xbuf[t_in:t_in + SUBLANES, :]

    small = small_ref[0]
    z = z_ref[0]
    if padded:
        row_valid = lax.broadcasted_iota(jnp.int32, (tt, 1), 0) < t_in
        qkv = jnp.where(row_valid, qkv, 0.0)
        small = jnp.concatenate([small, jnp.zeros((tt - t_in, LANES), F32)], axis=0)
        z = jnp.concatenate([z, jnp.zeros((tt - t_in, GDN_QK), F32)], axis=0)
    beta_all = _sigmoid(small)
    xa = small + gpar_ref[1:2, :]
    softplus = jnp.maximum(xa, 0.0) + jnp.log1p(jnp.exp(-jnp.abs(xa)))
    g_all = -jnp.exp(gpar_ref[0:1, :]) * softplus
    if padded:
        beta_all = jnp.where(row_valid, beta_all, 0.0)
        g_all = jnp.where(row_valid, g_all, 0.0)

    ri = lax.broadcasted_iota(jnp.int32, (tt, tt), 0)
    ci = lax.broadcasted_iota(jnp.int32, (tt, tt), 1)
    tril_blocks = ((ri >= ci) & (ri // c_rows == ci // c_rows)).astype(F32)
    gc_all = jnp.dot(tril_blocks, g_all, precision=lax.Precision.HIGHEST, preferred_element_type=F32)

    head_of_lane = lax.broadcasted_iota(jnp.int32, (1, GDN_QK), 1) // GDN_D

    def expand(cols):
        out = jnp.broadcast_to(cols[:, 0:1], (tt, GDN_QK))
        for h in range(1, GDN_H):
            out = jnp.where(head_of_lane == h, cols[:, h:h + 1], out)
        return out

    beta_x = expand(beta_all[:, 0:GDN_H])
    gc_x = expand(gc_all[:, GDN_H:2 * GDN_H])

    br = lax.broadcasted_iota(jnp.int32, (gw, gw), 0) // GDN_D
    bc = lax.broadcasted_iota(jnp.int32, (gw, gw), 1) // GDN_D
    block_mask = br == bc
    block_ones = block_mask.astype(BF16)

    def seg_sum(x):
        hi = x.astype(BF16)
        lo = (x - hi.astype(F32)).astype(BF16)
        return (jnp.dot(hi, block_ones, preferred_element_type=F32)
                + jnp.dot(lo, block_ones, preferred_element_type=F32))

    def block_diag(x):
        return jnp.where(block_mask, jnp.concatenate([x] * HEADS_PER_GROUP, axis=0), 0.0).astype(BF16)

    rr = lax.broadcasted_iota(jnp.int32, (c_rows, gw), 0)
    cc = lax.broadcasted_iota(jnp.int32, (c_rows, gw), 1) % c_rows
    incl, strict, diag = rr >= cc, rr > cc, rr == cc
    eye_t = diag.astype(F32)
    n_levels = int(math.log2(c_rows)) - 1

    n_chunks = tt // c_rows
    pre = {}
    for g in range(GDN_GROUPS):
        lanes = slice(g * gw, (g + 1) * gw)
        q_g = qkv[:, g * gw:(g + 1) * gw]
        k_g = qkv[:, GDN_QK + g * gw:GDN_QK + (g + 1) * gw]
        v_g = qkv[:, 2 * GDN_QK + g * gw:2 * GDN_QK + (g + 1) * gw]
        q_g = q_g * lax.rsqrt(seg_sum(q_g * q_g) + L2_EPS) * (GDN_D ** -0.5)
        k_g = k_g * lax.rsqrt(seg_sum(k_g * k_g) + L2_EPS)
        for c in range(n_chunks):
            rows = slice(c * c_rows, (c + 1) * c_rows)
            q_c, k_c, v_c = q_g[rows], k_g[rows], v_g[rows]
            b_c = beta_x[rows, lanes]
            gc_c = gc_x[rows, lanes]
            g_row = jnp.sum(jnp.where(diag, gc_c, 0.0), axis=0, keepdims=True)
            g_last = gc_c[c_rows - 1:c_rows, :]
            decay = jnp.exp(jnp.where(incl, gc_c - g_row, -jnp.inf))
            e_col = jnp.exp(gc_c)
            kb = k_c * b_c
            kk_qk = lax.dot_general(jnp.concatenate([kb, q_c], axis=0).astype(BF16), block_diag(k_c),
                                    _NT, preferred_element_type=F32)
            lmat = jnp.where(strict, kk_qk[0:c_rows] * decay, 0.0)
            a_mat = kk_qk[c_rows:] * decay
            inv = eye_t - lmat
            pw = _mm(lmat, block_diag(lmat))
            for lvl in range(n_levels):
                bd = block_diag(pw)
                if lvl < n_levels - 1:
                    both = _mm(jnp.concatenate([pw, inv], axis=0), bd)
                    pw = both[0:c_rows]
                    inv = inv + both[c_rows:]
                else:
                    inv = inv + _mm(inv, bd)
            inv_b = inv.astype(BF16)
            pre[g, c] = dict(
                u=_mm(inv_b, block_diag(v_c * b_c)),
                w=_mm(inv_b, block_diag(kb * e_col)),
                a=a_mat.astype(BF16),
                qg=q_c * e_col,
                kg=(k_c * jnp.exp(g_last - gc_c)).astype(BF16),
                s_decay=jnp.exp(g_last),
            )

    for g in range(GDN_GROUPS):
        lanes = slice(g * gw, (g + 1) * gw)
        s_bd = s_scr[g]
        for c in range(n_chunks):
            p = pre[g, c]
            both = _mm(jnp.concatenate([p["w"], p["qg"]], axis=0), s_bd)
            v_new = p["u"] - both[0:c_rows]
            o_buf[c * c_rows:(c + 1) * c_rows, lanes] = (
                both[c_rows:] + jnp.dot(p["a"], block_diag(v_new), preferred_element_type=F32))
            upd = lax.dot_general(p["kg"], v_new.astype(BF16), _TN, preferred_element_type=F32)
            s_bd = s_bd * p["s_decay"] + jnp.where(block_mask, upd, 0.0)
        s_scr[g] = s_bd

    norm_w = jnp.concatenate([normw_ref[...]] * HEADS_PER_GROUP, axis=1)
    outs = []
    for g in range(GDN_GROUPS):
        o_g = o_buf[:, g * gw:(g + 1) * gw]
        ms = seg_sum(o_g * o_g) * (1.0 / GDN_D)
        outs.append(o_g * lax.rsqrt(ms + RMS_EPS) * norm_w * _silu(z[:, g * gw:(g + 1) * gw]))
    o_ref[0] = jnp.concatenate(outs, axis=1)[0:t_in].astype(o_ref.dtype)

    @pl.when(t == pl.num_programs(1) - 1)
    def _():
        for h in range(GDN_H):
            g, hh = divmod(h, HEADS_PER_GROUP)
            sout_ref[0, h] = s_scr[g, hh * GDN_D:(hh + 1) * GDN_D, hh * GDN_D:(hh + 1) * GDN_D]


def _gdn(qkv, z, small, hist8, s0, conv_w, gpar, norm_w, t_in, act_dtype):
    b, t, _ = qkv.shape
    tt = max(t_in, GDN_CHUNK)
    seq = lambda i, j: (i, j, 0)
    per_b3 = lambda i, j: (i, 0, 0)
    per_b4 = lambda i, j: (i, 0, 0, 0)
    return pl.pallas_call(
        functools.partial(_gdn_kernel, t_in=t_in, tt=tt),
        grid=(b, t // t_in),
        in_specs=[
            pl.BlockSpec((1, t_in, GDN_QKV), seq), pl.BlockSpec((1, t_in, GDN_QK), seq),
            pl.BlockSpec((1, t_in, LANES), seq),
            pl.BlockSpec((1, SUBLANES, GDN_QKV), per_b3),
            pl.BlockSpec((1, GDN_H, GDN_D, GDN_D), per_b4),
            _const_spec(conv_w.shape), _const_spec(gpar.shape), _const_spec(norm_w.shape),
        ],
        out_specs=(pl.BlockSpec((1, t_in, GDN_QK), seq),
                   pl.BlockSpec((1, GDN_H, GDN_D, GDN_D), per_b4)),
        out_shape=(jax.ShapeDtypeStruct((b, t, GDN_QK), act_dtype),
                   jax.ShapeDtypeStruct((b, GDN_H, GDN_D, GDN_D), F32)),
        scratch_shapes=[pltpu.VMEM((tt + SUBLANES, GDN_QKV), F32),
                        pltpu.VMEM((GDN_GROUPS, GDN_GROUP, GDN_GROUP), F32),
                        pltpu.VMEM((tt, GDN_QK), F32)],
        compiler_params=pltpu.CompilerParams(
            dimension_semantics=("parallel", "arbitrary"), vmem_limit_bytes=VMEM_LIMIT_BYTES),
        name="gdn",
    )(qkv, z, small, hist8, s0, conv_w, gpar, norm_w)


def _attn_prompt_kernel(qi_ref, kj_ref, qt_ref, k_ref, ct_ref, w_uvt_ref, o_ref, m_scr, l_scr,
                        acc_scr, *, tq):
    p = pl.program_id(1)
    i = qi_ref[p]
    j = kj_ref[p]

    @pl.when(j == 0)
    def _():
        m_scr[...] = jnp.full(m_scr.shape, -jnp.inf, F32)
        l_scr[...] = jnp.zeros(l_scr.shape, F32)
        acc_scr[...] = jnp.zeros(acc_scr.shape, F32)

    def update(masked):
        k = k_ref[...]
        ct = ct_ref[...]
        if masked:
            key = lax.broadcasted_iota(jnp.int32, (tq, tq), 0)
            qry = lax.broadcasted_iota(jnp.int32, (tq, tq), 1)
            causal = key <= qry
        for h in range(MLA_H):
            s = jnp.dot(k, qt_ref[h], preferred_element_type=F32)
            if masked:
                s = jnp.where(causal, s, -jnp.inf)
            m_prev = m_scr[h:h + 1, :]
            m_new = jnp.maximum(m_prev, jnp.max(s, axis=0, keepdims=True))
            alpha = jnp.exp(m_prev - m_new)
            pmat = jnp.exp(s - m_new)
            l_scr[h:h + 1, :] = alpha * l_scr[h:h + 1, :] + jnp.sum(pmat, axis=0, keepdims=True)
            acc_scr[h] = alpha * acc_scr[h] + jnp.dot(ct, pmat.astype(BF16), preferred_element_type=F32)
            m_scr[h:h + 1, :] = m_new

    @pl.when(j < i)
    def _():
        update(False)

    @pl.when(j == i)
    def _():
        update(True)
        outs = []
        for h in range(MLA_H):
            o_lat_t = acc_scr[h] / l_scr[h:h + 1, :]
            outs.append(_mm(w_uvt_ref[h], o_lat_t))
        o_ref[...] = jnp.concatenate(outs, axis=0).T.astype(o_ref.dtype)


def _attn_prompt(q_t, k, c_t, w_uv_t, b, t, tq):
    nq = t // tq
    m = b * t
    pairs = [(i, j) for i in range(nq) for j in range(i + 1)]
    qi = jnp.asarray(np.array([p[0] for p in pairs], np.int32))
    kj = jnp.asarray(np.array([p[1] for p in pairs], np.int32))
    grid_spec = pltpu.PrefetchScalarGridSpec(
        num_scalar_prefetch=2,
        grid=(b, len(pairs)),
        in_specs=[
            pl.BlockSpec((MLA_H, MLA_QK, tq), lambda bi, p, qi, kj: (0, 0, bi * nq + qi[p])),
            pl.BlockSpec((tq, MLA_QK), lambda bi, p, qi, kj: (bi * nq + kj[p], 0)),
            pl.BlockSpec((MLA_KV_RANK, tq), lambda bi, p, qi, kj: (0, bi * nq + kj[p])),
            pl.BlockSpec(w_uv_t.shape, lambda bi, p, qi, kj: (0, 0, 0)),
        ],
        out_specs=pl.BlockSpec((tq, MLA_H * MLA_V_D), lambda bi, p, qi, kj: (bi * nq + qi[p], 0)),
        scratch_shapes=[pltpu.VMEM((MLA_H, tq), F32), pltpu.VMEM((MLA_H, tq), F32),
                        pltpu.VMEM((MLA_H, MLA_KV_RANK, tq), F32)],
    )
    return pl.pallas_call(
        functools.partial(_attn_prompt_kernel, tq=tq),
        grid_spec=grid_spec,
        out_shape=jax.ShapeDtypeStruct((m, MLA_H * MLA_V_D), BF16),
        compiler_params=pltpu.CompilerParams(
            dimension_semantics=("parallel", "arbitrary"), vmem_limit_bytes=VMEM_LIMIT_BYTES),
        name="attn_prompt",
    )(qi, kj, q_t, k, c_t, w_uv_t)


def _softmax_update(s, v_b, m_scr, l_scr, acc_scr):
    m_prev = m_scr[...]
    m_new = jnp.maximum(m_prev, jnp.max(s, axis=-1, keepdims=True))
    alpha = jnp.exp(m_prev - m_new)
    p = jnp.exp(s - m_new)
    l_scr[...] = alpha * l_scr[...] + jnp.sum(p, axis=-1, keepdims=True)
    acc_scr[...] = alpha * acc_scr[...] + jnp.dot(p.astype(BF16), v_b, preferred_element_type=F32)
    m_scr[...] = m_new


def _attn_sample_kernel(pt_ref, q_ref, knew_ref, *refs, pages_per_step, page, t_new):
    del pt_ref
    ckv_refs = refs[:pages_per_step]
    kpe_refs = refs[pages_per_step:2 * pages_per_step]
    w_uv_ref, o_ref, kcat, m_scr, l_scr, acc_scr = refs[2 * pages_per_step:]
    step = pl.program_id(1)
    rows = MLA_H * t_new

    @pl.when(step == 0)
    def _():
        m_scr[...] = jnp.full(m_scr.shape, -jnp.inf, F32)
        l_scr[...] = jnp.zeros(l_scr.shape, F32)
        acc_scr[...] = jnp.zeros(acc_scr.shape, F32)

    for p in range(pages_per_step):
        kcat[p * page:(p + 1) * page, 0:MLA_KV_RANK] = ckv_refs[p][...].astype(BF16)
        kcat[p * page:(p + 1) * page, MLA_KV_RANK:MLA_QK] = kpe_refs[p][...].astype(BF16)
    q = q_ref[...].reshape(rows, MLA_QK).astype(BF16)
    k_all = kcat[...]
    _softmax_update(lax.dot_general(q, k_all, _NT, preferred_element_type=F32),
                    k_all[:, 0:MLA_KV_RANK], m_scr, l_scr, acc_scr)

    @pl.when(step == pl.num_programs(1) - 1)
    def _():
        k_new = knew_ref[...]
        s = _mm_nt(q, k_new)
        qi = lax.broadcasted_iota(jnp.int32, (rows, t_new), 0) % t_new
        ki = lax.broadcasted_iota(jnp.int32, (rows, t_new), 1)
        s = jnp.where(ki <= qi, s, -jnp.inf)
        _softmax_update(s, k_new[:, 0:MLA_KV_RANK].astype(BF16), m_scr, l_scr, acc_scr)
        o_lat = acc_scr[...] / l_scr[...]
        outs = [_mm(o_lat[h * t_new:(h + 1) * t_new], w_uv_ref[h]) for h in range(MLA_H)]
        o_ref[...] = jnp.concatenate(outs, axis=-1).astype(o_ref.dtype)


def _attn_sample(q, k_new, cache_ckv, cache_kpe, layer, page_table, w_uv, b, t_new):
    n_pages = page_table.shape[1]
    page = cache_ckv.shape[2]
    pages_per_step = math.gcd(SAMPLE_PAGES_PER_STEP, n_pages)
    rows = MLA_H * t_new
    n_steps = n_pages // pages_per_step

    def page_map(p):
        return lambda bi, s, pt: (layer, pt[bi, s * pages_per_step + p], 0, 0)

    grid_spec = pltpu.PrefetchScalarGridSpec(
        num_scalar_prefetch=1,
        grid=(b, n_steps),
        in_specs=(
            [pl.BlockSpec((MLA_H, t_new, MLA_QK), lambda bi, s, pt: (0, bi, 0)),
             pl.BlockSpec((t_new, MLA_QK), lambda bi, s, pt: (bi, 0))]
            + [pl.BlockSpec((None, None, page, MLA_KV_RANK), page_map(p)) for p in range(pages_per_step)]
            + [pl.BlockSpec((None, None, page, MLA_ROPE_D), page_map(p)) for p in range(pages_per_step)]
            + [pl.BlockSpec(w_uv.shape, lambda bi, s, pt: (0, 0, 0))]
        ),
        out_specs=pl.BlockSpec((t_new, MLA_H * MLA_V_D), lambda bi, s, pt: (bi, 0)),
        scratch_shapes=[pltpu.VMEM((pages_per_step * page, MLA_QK), BF16),
                        pltpu.VMEM((rows, 1), F32), pltpu.VMEM((rows, 1), F32),
                        pltpu.VMEM((rows, MLA_KV_RANK), F32)],
    )
    return pl.pallas_call(
        functools.partial(_attn_sample_kernel, pages_per_step=pages_per_step, page=page, t_new=t_new),
        grid_spec=grid_spec,
        out_shape=jax.ShapeDtypeStruct((b * t_new, MLA_H * MLA_V_D), F32),
        compiler_params=pltpu.CompilerParams(
            dimension_semantics=("parallel", "arbitrary"), vmem_limit_bytes=VMEM_LIMIT_BYTES),
        name="attn_sample",
    )(page_table, q, k_new, *([cache_ckv] * pages_per_step), *([cache_kpe] * pages_per_step), w_uv)


def _ffn_kernel(x_ref, og_ref, om_ref, hist_ref, w_out_ref, ln_ref, w_up_ref, cw_ref, w_dn_ref,
                y_ref, nf_ref, *scratch, tm, alpha, short_seq):
    d_gdn = og_ref.shape[-1]
    n_chunks, _, two_fc = w_up_ref.shape
    fc = two_fc // 2
    mix = _mm(og_ref[...], w_out_ref[0:d_gdn, :]) + _mm(om_ref[...], w_out_ref[d_gdn:, :])
    x1 = _layernorm(alpha * x_ref[...] + mix, ln_ref[0:1, :], ln_ref[1:2, :])
    x1_b = x1.astype(BF16)

    if short_seq:
        nb = tm // SUBLANES
        t_pos = lax.broadcasted_iota(jnp.int32, (nb, SUBLANES, two_fc), 1)
    else:
        ubuf, carry = scratch

        @pl.when(pl.program_id(1) == 0)
        def _():
            carry[...] = hist_ref[0]

    acc = jnp.zeros((tm, x_ref.shape[-1]), F32)
    for c in range(n_chunks):
        cols = slice(c * two_fc, (c + 1) * two_fc)
        up = jnp.dot(x1_b, w_up_ref[c], preferred_element_type=F32)
        cw = cw_ref[c]
        if short_seq:
            up3 = up.reshape(nb, SUBLANES, two_fc)
            h0 = hist_ref[:, SUBLANES - 2:SUBLANES - 1, cols]
            h1 = hist_ref[:, SUBLANES - 1:SUBLANES, cols]
            prev1 = jnp.where(t_pos == 0, h1, pltpu.roll(up3, 1, axis=1))
            prev2 = jnp.where(t_pos == 0, h0, jnp.where(t_pos == 1, h1, pltpu.roll(up3, 2, axis=1)))
            hc = (prev2 * cw[0:1] + prev1 * cw[1:2] + up3 * cw[2:3]).reshape(tm, two_fc)
            nf_ref[:, :, cols] = up3
        else:
            ubuf[0:SUBLANES, :] = carry[:, cols]
            ubuf[SUBLANES:SUBLANES + tm, :] = up
            hc = (ubuf[pl.ds(SUBLANES - 2, tm), :] * cw[0:1]
                  + ubuf[pl.ds(SUBLANES - 1, tm), :] * cw[1:2] + up * cw[2:3])
            last = ubuf[tm:tm + SUBLANES, :]
            carry[:, cols] = last
            nf_ref[0, :, cols] = last
        hmid = _silu(hc[:, fc:]) * hc[:, 0:fc]
        acc = acc + _mm(hmid, w_dn_ref[c])
    y_ref[...] = _layernorm(alpha * x1 + acc, ln_ref[2:3, :], ln_ref[3:4, :])


def _ffn(x2, o_gdn, o_mla, hist8, wts, b, t, tm, alpha):
    m, d = x2.shape
    n_chunks, _, two_fc = wts["w_up"].shape
    two_ff = n_chunks * two_fc
    short_seq = t == SUBLANES
    if short_seq:
        grid = (m // tm, 1)
        nb = tm // SUBLANES
        row = lambda i, j: (i, 0)
        hist_spec = pl.BlockSpec((nb, SUBLANES, two_ff), lambda i, j: (i, 0, 0))
        scratch = []
    else:
        nt = t // tm
        grid = (b, nt)
        row = lambda i, j: (i * nt + j, 0)
        hist_spec = pl.BlockSpec((1, SUBLANES, two_ff), lambda i, j: (i, 0, 0))
        scratch = [pltpu.VMEM((tm + SUBLANES, two_fc), F32), pltpu.VMEM((SUBLANES, two_ff), F32)]
    return pl.pallas_call(
        functools.partial(_ffn_kernel, tm=tm, alpha=alpha, short_seq=short_seq),
        grid=grid,
        in_specs=[
            pl.BlockSpec((tm, d), row), pl.BlockSpec((tm, o_gdn.shape[1]), row),
            pl.BlockSpec((tm, o_mla.shape[1]), row), hist_spec,
            _const_spec(wts["w_out"].shape), _const_spec(wts["ln"].shape),
            _const_spec(wts["w_up"].shape), _const_spec(wts["ffn_cw"].shape),
            _const_spec(wts["w_dn"].shape),
        ],
        out_specs=(pl.BlockSpec((tm, d), row), hist_spec),
        out_shape=(jax.ShapeDtypeStruct((m, d), F32),
                   jax.ShapeDtypeStruct((b, SUBLANES, two_ff), F32)),
        scratch_shapes=scratch,
        compiler_params=pltpu.CompilerParams(
            dimension_semantics=("parallel", "arbitrary"), vmem_limit_bytes=VMEM_LIMIT_BYTES),
        name="ffn",
    )(x2, o_gdn, o_mla, hist8, wts["w_out"], wts["ln"], wts["w_up"], wts["ffn_cw"], wts["w_dn"])


def _swap_halves(w):
    half = w.shape[-1] // 2
    return jnp.concatenate([-w[..., half:], w[..., :half]], axis=-1)


def _prep_weights(w_in, gdn_conv_w, gdn_A_log, gdn_dt_bias, gdn_norm_w, mla_q_norm_w, mla_w_uq,
                  mla_kv_norm_w, mla_w_uk, mla_w_uv, w_out, ln1_g, ln1_b, ffn_w_up, ffn_conv_w,
                  ffn_w_down, ln2_g, ln2_b):
    d_model = w_in.shape[0]
    o = 0
    w_qkv = w_in[:, o:o + GDN_QKV]; o += GDN_QKV
    w_z = w_in[:, o:o + GDN_QK]; o += GDN_QK
    w_b = w_in[:, o:o + GDN_H]; o += GDN_H
    w_a = w_in[:, o:o + GDN_H]; o += GDN_H
    w_cq = w_in[:, o:o + MLA_Q_RANK]; o += MLA_Q_RANK
    w_ckv = w_in[:, o:o + MLA_KV_RANK]; o += MLA_KV_RANK
    w_kpe = w_in[:, o:o + MLA_ROPE_D]
    pad = jnp.zeros((d_model, LANES - 2 * GDN_H - 2 * MLA_ROPE_D), w_in.dtype)
    w_small = jnp.concatenate([w_b, w_a, w_kpe, _swap_halves(w_kpe), pad], axis=1)
    w_rest = jnp.concatenate([w_cq, w_ckv, w_small], axis=1)

    uq = mla_w_uq.reshape(MLA_Q_RANK, MLA_H, MLA_NOPE_D + MLA_ROPE_D)
    uq_nope = uq[:, :, :MLA_NOPE_D].reshape(MLA_Q_RANK, MLA_H * MLA_NOPE_D)
    uq_pe = uq[:, :, MLA_NOPE_D:]
    w_q = jnp.concatenate([uq_nope, uq_pe.reshape(MLA_Q_RANK, -1),
                           _swap_halves(uq_pe).reshape(MLA_Q_RANK, -1)], axis=1)

    gpar = jnp.zeros((2, LANES), F32)
    gpar = gpar.at[0, GDN_H:2 * GDN_H].set(gdn_A_log).at[1, GDN_H:2 * GDN_H].set(gdn_dt_bias)

    d_ff = ffn_w_down.shape[0]
    n_chunks = d_ff // FFN_CHUNK

    def interleave(w):
        lead = w.shape[:-1]
        w = w.reshape(lead + (2, n_chunks, FFN_CHUNK))
        return jnp.moveaxis(w, -3, -2).reshape(lead + (n_chunks, 2 * FFN_CHUNK))

    return {
        "w_qkv": w_qkv.astype(BF16), "w_z": w_z.astype(BF16), "w_rest": w_rest.astype(BF16),
        "w_q": w_q.astype(BF16), "w_q_t": w_q.T.astype(BF16),
        "w_uk_t": jnp.transpose(mla_w_uk, (1, 2, 0)).astype(BF16),
        "w_uk": jnp.transpose(mla_w_uk, (1, 0, 2)).astype(BF16),
        "w_uv": jnp.transpose(mla_w_uv, (1, 0, 2)).astype(BF16),
        "w_uv_t": jnp.transpose(mla_w_uv, (1, 2, 0)).astype(BF16),
        "q_norm": mla_q_norm_w.reshape(1, -1), "kv_norm": mla_kv_norm_w.reshape(1, -1),
        "gdn_conv_w": gdn_conv_w, "gpar": gpar, "gdn_norm_w": gdn_norm_w.reshape(1, -1),
        "w_out": w_out.astype(BF16),
        "ln": jnp.stack([ln1_g, ln1_b, ln2_g, ln2_b]),
        "w_up": jnp.transpose(interleave(ffn_w_up), (1, 0, 2)).astype(BF16),
        "ffn_cw": jnp.transpose(interleave(ffn_conv_w), (1, 0, 2)),
        "w_dn": ffn_w_down.reshape(n_chunks, FFN_CHUNK, -1).astype(BF16),
        "interleave": interleave, "n_chunks": n_chunks,
    }


def _rope_tables(past_len, t, tm, transposed):
    half = MLA_ROPE_D // 2
    pos = (past_len + jnp.arange(t, dtype=jnp.int32)).astype(F32)
    inv = ROPE_BASE ** (-jnp.arange(half, dtype=F32) / half)
    ang = pos[:, None] * inv
    reps = (max(tm // t, 1), 2 * MLA_H)
    cos, sin = jnp.tile(jnp.cos(ang), reps), jnp.tile(jnp.sin(ang), reps)
    if transposed:
        return cos.T, sin.T, cos[:, 0:MLA_ROPE_D], sin[:, 0:MLA_ROPE_D]
    return cos, sin


def _pad_hist(h):
    return jnp.pad(h, ((0, 0), (SUBLANES - h.shape[1], 0), (0, 0)))


def _layer(x, past, layer, s0, conv_hist, ffn_hist, wts, alpha, *, tm_proj, tm_ffn, tq, gdn_rows):
    b, t, d = x.shape
    m = b * t
    x2 = x.reshape(m, d)
    short_seq = past is not None
    past_len = past[2].shape[1] * past[0].shape[2] if short_seq else 0
    act_dtype = F32 if short_seq else BF16
    rope = _rope_tables(past_len, t, tm_proj, transposed=not short_seq)
    qkv, z, small, c_new, kpe_new, q, k, *c_t = _in_proj(x2, wts, rope, tm_proj,
                                                         transposed_q=not short_seq)

    o_gdn, s_new = _gdn(qkv.reshape(b, t, -1), z.reshape(b, t, -1), small.reshape(b, t, -1),
                        _pad_hist(conv_hist), s0, wts["gdn_conv_w"], wts["gpar"], wts["gdn_norm_w"],
                        gdn_rows, act_dtype)
    if short_seq:
        cache_ckv, cache_kpe, page_table = past
        o_mla = _attn_sample(q, k, cache_ckv, cache_kpe, layer, page_table, wts["w_uv"], b, t)
    else:
        o_mla = _attn_prompt(q, k, c_t[0], wts["w_uv_t"], b, t, tq)

    hist8 = wts["interleave"](_pad_hist(ffn_hist)).reshape(b, SUBLANES, -1)
    y, nf8 = _ffn(x2, o_gdn.reshape(m, -1), o_mla, hist8, wts, b, t, tm_ffn, alpha)

    n_chunks = wts["n_chunks"]
    nf = nf8[:, SUBLANES - (FFN_CONV_TAPS - 1):, :].reshape(b, FFN_CONV_TAPS - 1, n_chunks, 2, FFN_CHUNK)
    new_ffn = jnp.moveaxis(nf, -2, -3).reshape(b, FFN_CONV_TAPS - 1, -1)
    new_conv = qkv.reshape(b, t, -1)[:, t - (GDN_CONV_TAPS - 1):, :]
    return (y.reshape(b, t, d), c_new.reshape(b, t, -1), kpe_new.reshape(b, t, -1), s_new,
            new_conv, new_ffn)


def kernel(x_prompt, x_sample, cache_ckv, cache_kpe, page_table, state_gdn, state_gdn_conv,
           state_ffn_conv, w_in, gdn_conv_w, gdn_A_log, gdn_dt_bias, gdn_norm_w, mla_q_norm_w,
           mla_w_uq, mla_kv_norm_w, mla_w_uk, mla_w_uv, w_out, ln1_g, ln1_b, ffn_w_up, ffn_conv_w,
           ffn_w_down, ln2_g, ln2_b):
    depth = w_in.shape[0]
    alpha = (2.0 * depth) ** 0.25
    bp, tp, _ = x_prompt.shape
    bs, ts, _ = x_sample.shape
    assert ts == SUBLANES, "sample group: one 8-row tile per sequence"
    xp, xs = x_prompt, x_sample
    new_p, new_s = [], []
    for l in range(depth):
        wts = _prep_weights(w_in[l], gdn_conv_w[l], gdn_A_log[l], gdn_dt_bias[l], gdn_norm_w[l],
                            mla_q_norm_w[l], mla_w_uq[l], mla_kv_norm_w[l], mla_w_uk[l],
                            mla_w_uv[l], w_out[l], ln1_g[l], ln1_b[l], ffn_w_up[l], ffn_conv_w[l],
                            ffn_w_down[l], ln2_g[l], ln2_b[l])
        two_ff = ffn_w_up.shape[-1]
        xp, *st_p = _layer(
            xp, None, l, jnp.zeros((bp, GDN_H, GDN_D, GDN_D), F32),
            jnp.zeros((bp, GDN_CONV_TAPS - 1, GDN_QKV), F32),
            jnp.zeros((bp, FFN_CONV_TAPS - 1, two_ff), F32), wts, alpha,
            tm_proj=min(256, tp), tm_ffn=min(256, tp), tq=min(256, tp), gdn_rows=min(256, tp))
        xs, *st_s = _layer(
            xs, (cache_ckv, cache_kpe, page_table), l, state_gdn[l], state_gdn_conv[l],
            state_ffn_conv[l], wts, alpha,
            tm_proj=min(256, bs * ts), tm_ffn=min(256, bs * ts), tq=None, gdn_rows=ts)
        new_p.append(st_p)
        new_s.append(st_s)
    stack = lambda sts: [jnp.stack(v) for v in zip(*sts)]
    return (xp, xs, *stack(new_p), *stack(new_s))
```

```python
import functools
import math

import jax
import jax.numpy as jnp
import numpy as np
from jax import lax
from jax.experimental import pallas as pl
from jax.experimental.pallas import tpu as pltpu

F32 = jnp.float32
BF16 = jnp.bfloat16

GDN_H = 8
GDN_D = 64
GDN_QK = GDN_H * GDN_D
GDN_QKV = 3 * GDN_QK
GDN_CONV_TAPS = 4
GDN_CHUNK = 64
GDN_GROUP = 256
GDN_GROUPS = GDN_QK // GDN_GROUP
HEADS_PER_GROUP = GDN_GROUP // GDN_D
MLA_H = 8
MLA_Q_RANK = 256
MLA_KV_RANK = 128
MLA_NOPE_D = 64
MLA_ROPE_D = 32
MLA_V_D = 64
MLA_QK = MLA_KV_RANK + MLA_ROPE_D
MLA_SOFTMAX_SCALE = (MLA_NOPE_D + MLA_ROPE_D) ** -0.5
LOG2_E = math.log2(math.e)
CT_ROWS = MLA_KV_RANK + 16
ROPE_BASE = 10000.0
FFN_CONV_TAPS = 3
RMS_EPS = 1e-6
LN_EPS = 1e-5
L2_EPS = 1e-6

SUBLANES = 8
LANES = 128
VMEM_LIMIT_BYTES = 56 * 1024 * 1024
FFN_CHUNK = 256
SAMPLE_PAGES_PER_STEP = 16

_NT = (((1,), (1,)), ((), ()))
_TN = (((0,), (0,)), ((), ()))


def _mm(a, b):
    return jnp.dot(a.astype(BF16), b.astype(BF16), preferred_element_type=F32)


def _mm_nt(a, b):
    return lax.dot_general(a.astype(BF16), b.astype(BF16), _NT, preferred_element_type=F32)


def _sigmoid(x):
    return 1.0 / (1.0 + jnp.exp(-x))


def _silu(x):
    return x * _sigmoid(x)


def _rms(x, w):
    return x * lax.rsqrt(jnp.mean(x * x, axis=-1, keepdims=True) + RMS_EPS) * w


def _layernorm(x, g, b):
    mu = jnp.mean(x, axis=-1, keepdims=True)
    xc = x - mu
    var = jnp.mean(xc * xc, axis=-1, keepdims=True)
    return xc * lax.rsqrt(var + LN_EPS) * g + b


def _eye(n, dtype):
    return (lax.broadcasted_iota(jnp.int32, (n, n), 0)
            == lax.broadcasted_iota(jnp.int32, (n, n), 1)).astype(dtype)


def _const_spec(shape):
    nd = len(shape)
    return pl.BlockSpec(shape, lambda *_: (0,) * nd, pipeline_mode=pl.Buffered(1))


def _in_proj_kernel(x_ref, w_qkv_ref, w_z_ref, w_rest_ref, w_q_ref, w_uk_ref, qn_ref, kvn_ref,
                    *refs, transposed_q):
    if transposed_q:
        (cos_ref, sin_ref, cos_k_ref, sin_k_ref,
         qkv_ref, z_ref, small_ref, ckv_ref, kpe_ref, q_ref, k_ref, ct_ref) = refs
    else:
        cos_ref, sin_ref, qkv_ref, z_ref, small_ref, ckv_ref, kpe_ref, q_ref, k_ref = refs
    xb = x_ref[...].astype(BF16)
    qkv_ref[...] = jnp.dot(xb, w_qkv_ref[...], preferred_element_type=F32)
    z_ref[...] = jnp.dot(xb, w_z_ref[...], preferred_element_type=F32)
    rest = jnp.dot(xb, w_rest_ref[...], preferred_element_type=F32)
    cq = rest[:, 0:MLA_Q_RANK]
    ckv_raw = rest[:, MLA_Q_RANK:MLA_Q_RANK + MLA_KV_RANK]
    small = rest[:, MLA_Q_RANK + MLA_KV_RANK:]
    small_ref[...] = small

    c_new = _rms(ckv_raw, kvn_ref[...])
    ckv_ref[...] = c_new
    cqn = _rms(cq, qn_ref[...])
    n_nope = MLA_H * MLA_NOPE_D
    npe = MLA_H * MLA_ROPE_D
    k_rot_a = small[:, 16:16 + MLA_ROPE_D]
    k_rot_b = small[:, 16 + MLA_ROPE_D:16 + 2 * MLA_ROPE_D]

    if transposed_q:
        cos_t = cos_ref[...]
        sin_t = sin_ref[...]
        tm = x_ref.shape[0]
        kpe = k_rot_a * cos_k_ref[...] + k_rot_b * sin_k_ref[...]
        ct_ref[0:MLA_KV_RANK, :] = lax.dot_general(_eye(LANES, BF16), c_new.astype(BF16), _NT,
                                                   preferred_element_type=F32).astype(ct_ref.dtype)
        ct_ref[MLA_KV_RANK:, :] = jnp.ones((CT_ROWS - MLA_KV_RANK, tm), ct_ref.dtype)
        q_t = lax.dot_general(w_q_ref[...], cqn.astype(BF16), _NT,
                              preferred_element_type=F32)
        scale = MLA_SOFTMAX_SCALE * LOG2_E
        q_pe = (q_t[n_nope:n_nope + npe] * cos_t + q_t[n_nope + npe:] * sin_t) * scale
        for h in range(MLA_H):
            q_lat = _mm(w_uk_ref[h], q_t[h * MLA_NOPE_D:(h + 1) * MLA_NOPE_D]) * scale
            cols = slice(h * tm, (h + 1) * tm)
            q_ref[0:MLA_KV_RANK, cols] = q_lat.astype(q_ref.dtype)
            q_ref[MLA_KV_RANK:MLA_QK, cols] = q_pe[h * MLA_ROPE_D:(h + 1) * MLA_ROPE_D].astype(q_ref.dtype)
    else:
        cos = cos_ref[...]
        sin = sin_ref[...]
        kpe = k_rot_a * cos[:, 0:MLA_ROPE_D] + k_rot_b * sin[:, 0:MLA_ROPE_D]
        q = _mm(cqn, w_q_ref[...])
        q_pe = (q[:, n_nope:n_nope + npe] * cos + q[:, n_nope + npe:] * sin) * MLA_SOFTMAX_SCALE
        for h in range(MLA_H):
            q_lat = _mm(q[:, h * MLA_NOPE_D:(h + 1) * MLA_NOPE_D], w_uk_ref[h]) * MLA_SOFTMAX_SCALE
            q_ref[h, :, 0:MLA_KV_RANK] = q_lat.astype(q_ref.dtype)
            q_ref[h, :, MLA_KV_RANK:MLA_QK] = q_pe[:, h * MLA_ROPE_D:(h + 1) * MLA_ROPE_D].astype(q_ref.dtype)

    kpe_ref[...] = kpe
    k_ref[:, 0:MLA_KV_RANK] = c_new.astype(k_ref.dtype)
    k_ref[:, MLA_KV_RANK:MLA_QK] = kpe.astype(k_ref.dtype)


def _in_proj(x2, wts, rope, tm, transposed_q):
    m, d = x2.shape
    row = lambda i: (i, 0)
    npe = MLA_H * MLA_ROPE_D
    act_dtype = BF16 if transposed_q else F32
    out_shape = [
        jax.ShapeDtypeStruct((m, GDN_QKV), F32),
        jax.ShapeDtypeStruct((m, GDN_QK), F32),
        jax.ShapeDtypeStruct((m, LANES), F32),
        jax.ShapeDtypeStruct((m, MLA_KV_RANK), F32),
        jax.ShapeDtypeStruct((m, MLA_ROPE_D), F32),
    ]
    out_specs = [
        pl.BlockSpec((tm, GDN_QKV), row), pl.BlockSpec((tm, GDN_QK), row),
        pl.BlockSpec((tm, LANES), row), pl.BlockSpec((tm, MLA_KV_RANK), row),
        pl.BlockSpec((tm, MLA_ROPE_D), row),
    ]
    if transposed_q:
        n_rope_tiles = rope[0].shape[1] // tm
        rope_specs = [pl.BlockSpec((npe, tm), lambda i: (0, i % n_rope_tiles))] * 2
        rope_specs += [pl.BlockSpec((tm, MLA_ROPE_D), lambda i: (i % n_rope_tiles, 0))] * 2
        w_q, w_uk = wts["w_q_t"], wts["w_uk"]
        out_shape += [jax.ShapeDtypeStruct((m // tm, MLA_QK, MLA_H * tm), act_dtype),
                      jax.ShapeDtypeStruct((m, MLA_QK), act_dtype),
                      jax.ShapeDtypeStruct((CT_ROWS, m), act_dtype)]
        out_specs += [pl.BlockSpec((None, MLA_QK, MLA_H * tm), lambda i: (i, 0, 0)),
                      pl.BlockSpec((tm, MLA_QK), row),
                      pl.BlockSpec((CT_ROWS, tm), lambda i: (0, i))]
    else:
        n_rope_tiles = rope[0].shape[0] // tm
        rope_specs = [pl.BlockSpec((tm, npe), lambda i: (i % n_rope_tiles, 0))] * 2
        w_q, w_uk = wts["w_q"], wts["w_uk_t"]
        out_shape += [jax.ShapeDtypeStruct((MLA_H, m, MLA_QK), act_dtype),
                      jax.ShapeDtypeStruct((m, MLA_QK), act_dtype)]
        out_specs += [pl.BlockSpec((MLA_H, tm, MLA_QK), lambda i: (0, i, 0)),
                      pl.BlockSpec((tm, MLA_QK), row)]
    return pl.pallas_call(
        functools.partial(_in_proj_kernel, transposed_q=transposed_q),
        grid=(m // tm,),
        in_specs=[
            pl.BlockSpec((tm, d), row),
            _const_spec(wts["w_qkv"].shape), _const_spec(wts["w_z"].shape),
            _const_spec(wts["w_rest"].shape), _const_spec(w_q.shape), _const_spec(w_uk.shape),
            _const_spec(wts["q_norm"].shape), _const_spec(wts["kv_norm"].shape),
            *rope_specs,
        ],
        out_specs=tuple(out_specs),
        out_shape=tuple(out_shape),
        compiler_params=pltpu.CompilerParams(
            dimension_semantics=("parallel",), vmem_limit_bytes=VMEM_LIMIT_BYTES),
        name="in_proj",
    )(x2, wts["w_qkv"], wts["w_z"], wts["w_rest"], w_q, w_uk, wts["q_norm"], wts["kv_norm"],
      *rope)


def _gdn_kernel(qkv_ref, z_ref, small_ref, hist_ref, s0_ref, convw_ref, gpar_ref, normw_ref,
                o_ref, sout_ref, xbuf, s_scr, o_buf, *, t_in, tt):
    t = pl.program_id(1)
    c_rows, gw = GDN_CHUNK, GDN_GROUP
    padded = tt > t_in

    @pl.when(t == 0)
    def _():
        xbuf[0:SUBLANES, :] = hist_ref[0]
        s_scr[...] = jnp.zeros(s_scr.shape, F32)
        for h in range(GDN_H):
            g, hh = divmod(h, HEADS_PER_GROUP)
            s_scr[g, hh * GDN_D:(hh + 1) * GDN_D, hh * GDN_D:(hh + 1) * GDN_D] = s0_ref[0, h]

    if padded:
        xbuf[SUBLANES + t_in:SUBLANES + tt, :] = jnp.zeros((tt - t_in, GDN_QKV), F32)
    xbuf[SUBLANES:SUBLANES + t_in, :] = qkv_ref[0]
    cw = convw_ref[...]
    y = xbuf[pl.ds(SUBLANES - 3, tt), :] * cw[0:1]
    for j in range(1, GDN_CONV_TAPS):
        y = y + xbuf[pl.ds(SUBLANES - 3 + j, tt), :] * cw[j:j + 1]
    qkv = _silu(y)
    xbuf[0:SUBLANES, :] = xbuf[t_in:t_in + SUBLANES, :]

    small = small_ref[0]
    z = z_ref[0]
    if padded:
        row_valid = lax.broadcasted_iota(jnp.int32, (tt, 1), 0) < t_in
        qkv = jnp.where(row_valid, qkv, 0.0)
        small = jnp.concatenate([small, jnp.zeros((tt - t_in, LANES), F32)], axis=0)
        z = jnp.concatenate([z, jnp.zeros((tt - t_in, GDN_QK), F32)], axis=0)
    beta_all = _sigmoid(small)
    xa = small + gpar_ref[1:2, :]
    softplus = jnp.maximum(xa, 0.0) + jnp.log1p(jnp.exp(-jnp.abs(xa)))
    g_all = -jnp.exp(gpar_ref[0:1, :]) * softplus
    if padded:
        beta_all = jnp.where(row_valid, beta_all, 0.0)
        g_all = jnp.where(row_valid, g_all, 0.0)

    def split3(x):
        hi = x.astype(BF16)
        r1 = x - hi.astype(F32)
        mid = r1.astype(BF16)
        return hi, mid, (r1 - mid.astype(F32)).astype(BF16)

    def mm_exact(x, ones_mat, terms=3):
        return sum(jnp.dot(part, ones_mat, preferred_element_type=F32) for part in split3(x)[:terms])

    lane = lax.broadcasted_iota(jnp.int32, (LANES, GDN_QK), 0)
    head = lax.broadcasted_iota(jnp.int32, (LANES, GDN_QK), 1) // GDN_D
    beta_x = mm_exact(beta_all, (lane == head).astype(BF16))
    g_x = mm_exact(g_all, (lane == head + GDN_H).astype(BF16))
    ri = lax.broadcasted_iota(jnp.int32, (tt, tt), 0)
    ci = lax.broadcasted_iota(jnp.int32, (tt, tt), 1)
    tril_blocks = ((ri >= ci) & (ri // c_rows == ci // c_rows)).astype(BF16)
    gc_x = sum(jnp.dot(tril_blocks, part, preferred_element_type=F32) for part in split3(g_x))

    br = lax.broadcasted_iota(jnp.int32, (gw, gw), 0) // GDN_D
    bc = lax.broadcasted_iota(jnp.int32, (gw, gw), 1) // GDN_D
    block_mask = br == bc
    block_ones = block_mask.astype(BF16)

    def block_diag(x):
        return jnp.concatenate([x.astype(BF16)] * HEADS_PER_GROUP, axis=0) * block_ones

    rr = lax.broadcasted_iota(jnp.int32, (c_rows, gw), 0)
    cc = lax.broadcasted_iota(jnp.int32, (c_rows, gw), 1) % c_rows
    incl, strict, diag = rr >= cc, rr > cc, rr == cc
    eye_t = diag.astype(F32)
    n_levels = int(math.log2(c_rows)) - 1
    n_chunks = tt // c_rows

    q_n, k_n = [], []
    for g in range(GDN_GROUPS):
        q_g = qkv[:, g * gw:(g + 1) * gw]
        k_g = qkv[:, GDN_QK + g * gw:GDN_QK + (g + 1) * gw]
        q_n.append(q_g * lax.rsqrt(mm_exact(q_g * q_g, block_ones, 2) + L2_EPS) * (GDN_D ** -0.5))
        k_n.append(k_g * lax.rsqrt(mm_exact(k_g * k_g, block_ones, 2) + L2_EPS))

    units = [(g, c) for c in range(n_chunks) for g in range(GDN_GROUPS)]
    st = {}
    for g, c in units:
        rows = slice(c * c_rows, (c + 1) * c_rows)
        lanes = slice(g * gw, (g + 1) * gw)
        q_c, k_c = q_n[g][rows], k_n[g][rows]
        v_c = qkv[rows, 2 * GDN_QK + g * gw:2 * GDN_QK + (g + 1) * gw]
        b_c = beta_x[rows, lanes]
        gc_c = gc_x[rows, lanes]
        g_row = jnp.sum(jnp.where(diag, gc_c, 0.0), axis=0, keepdims=True)
        g_last = gc_c[c_rows - 1:c_rows, :]
        decay = jnp.exp(jnp.where(incl, gc_c - g_row, -jnp.inf))
        e_col = jnp.exp(gc_c)
        kb = k_c * b_c
        kk_qk = lax.dot_general(jnp.concatenate([kb, q_c], axis=0).astype(BF16), block_diag(k_c),
                                _NT, preferred_element_type=F32)
        lmat = jnp.where(strict, kk_qk[0:c_rows] * decay, 0.0)
        st[g, c] = dict(
            inv=eye_t - lmat, lmat=lmat,
            a=(kk_qk[c_rows:] * decay).astype(BF16),
            vb=block_diag(v_c * b_c), kbe=block_diag(kb * e_col),
            qg=q_c * e_col,
            kg=(k_c * jnp.exp(g_last - gc_c)).astype(BF16),
            s_decay=jnp.exp(g_last),
        )
    for u in units:
        st[u]["pw"] = _mm(st[u]["lmat"], block_diag(st[u]["lmat"]))
    for lvl in range(n_levels):
        for u in units:
            s_u = st[u]
            bd = block_diag(s_u["pw"])
            if lvl < n_levels - 1:
                both = _mm(jnp.concatenate([s_u["pw"], s_u["inv"]], axis=0), bd)
                s_u["pw"] = both[0:c_rows]
                s_u["inv"] = s_u["inv"] + both[c_rows:]
            else:
                s_u["inv"] = s_u["inv"] + _mm(s_u["inv"], bd)
    for u in units:
        inv_b = st[u]["inv"].astype(BF16)
        st[u]["u"] = jnp.dot(inv_b, st[u]["vb"], preferred_element_type=F32)
        st[u]["w"] = jnp.dot(inv_b, st[u]["kbe"], preferred_element_type=F32)

    s_bd = [s_scr[g] for g in range(GDN_GROUPS)]
    for g, c in units:
        p = st[g, c]
        both = _mm(jnp.concatenate([p["w"], p["qg"]], axis=0), s_bd[g])
        v_new = p["u"] - both[0:c_rows]
        o_buf[c * c_rows:(c + 1) * c_rows, g * gw:(g + 1) * gw] = (
            both[c_rows:] + jnp.dot(p["a"], block_diag(v_new), preferred_element_type=F32))
        upd = lax.dot_general(p["kg"], v_new.astype(BF16), _TN, preferred_element_type=F32)
        s_bd[g] = s_bd[g] * p["s_decay"] + jnp.where(block_mask, upd, 0.0)
    for g in range(GDN_GROUPS):
        s_scr[g] = s_bd[g]

    norm_w = jnp.concatenate([normw_ref[...]] * HEADS_PER_GROUP, axis=1)
    outs = []
    for g in range(GDN_GROUPS):
        o_g = o_buf[:, g * gw:(g + 1) * gw]
        ms = mm_exact(o_g * o_g, block_ones, 2) * (1.0 / GDN_D)
        outs.append(o_g * lax.rsqrt(ms + RMS_EPS) * norm_w * _silu(z[:, g * gw:(g + 1) * gw]))
    o_ref[0] = jnp.concatenate(outs, axis=1)[0:t_in].astype(o_ref.dtype)

    @pl.when(t == pl.num_programs(1) - 1)
    def _():
        for h in range(GDN_H):
            g, hh = divmod(h, HEADS_PER_GROUP)
            sout_ref[0, h] = s_scr[g, hh * GDN_D:(hh + 1) * GDN_D, hh * GDN_D:(hh + 1) * GDN_D]


def _gdn(qkv, z, small, hist8, s0, conv_w, gpar, norm_w, t_in, act_dtype):
    b, t, _ = qkv.shape
    tt = max(t_in, GDN_CHUNK)
    seq = lambda i, j: (i, j, 0)
    per_b3 = lambda i, j: (i, 0, 0)
    per_b4 = lambda i, j: (i, 0, 0, 0)
    return pl.pallas_call(
        functools.partial(_gdn_kernel, t_in=t_in, tt=tt),
        grid=(b, t // t_in),
        in_specs=[
            pl.BlockSpec((1, t_in, GDN_QKV), seq), pl.BlockSpec((1, t_in, GDN_QK), seq),
            pl.BlockSpec((1, t_in, LANES), seq),
            pl.BlockSpec((1, SUBLANES, GDN_QKV), per_b3),
            pl.BlockSpec((1, GDN_H, GDN_D, GDN_D), per_b4),
            _const_spec(conv_w.shape), _const_spec(gpar.shape), _const_spec(norm_w.shape),
        ],
        out_specs=(pl.BlockSpec((1, t_in, GDN_QK), seq),
                   pl.BlockSpec((1, GDN_H, GDN_D, GDN_D), per_b4)),
        out_shape=(jax.ShapeDtypeStruct((b, t, GDN_QK), act_dtype),
                   jax.ShapeDtypeStruct((b, GDN_H, GDN_D, GDN_D), F32)),
        scratch_shapes=[pltpu.VMEM((tt + SUBLANES, GDN_QKV), F32),
                        pltpu.VMEM((GDN_GROUPS, GDN_GROUP, GDN_GROUP), F32),
                        pltpu.VMEM((tt, GDN_QK), F32)],
        compiler_params=pltpu.CompilerParams(
            dimension_semantics=("parallel", "arbitrary"), vmem_limit_bytes=VMEM_LIMIT_BYTES),
        name="gdn",
    )(qkv, z, small, hist8, s0, conv_w, gpar, norm_w)


def _attn_prompt_kernel(qi_ref, kj_ref, qt_ref, k_ref, ct_ref, w_uvt_ref, o_ref, m_scr, acc_scr,
                        *, tq):
    p = pl.program_id(1)
    i = qi_ref[p]
    j = kj_ref[p]

    @pl.when(j == 0)
    def _():
        m_scr[...] = jnp.full(m_scr.shape, -jnp.inf, F32)
        acc_scr[...] = jnp.zeros(acc_scr.shape, F32)

    def update(masked):
        s = jnp.dot(k_ref[...], qt_ref[...], preferred_element_type=F32)
        if masked:
            key = lax.broadcasted_iota(jnp.int32, s.shape, 0)
            qry = lax.broadcasted_iota(jnp.int32, s.shape, 1) % tq
            s = jnp.where(key <= qry, s, -jnp.inf)
        m_prev = m_scr[...]
        m_new = jnp.maximum(m_prev, jnp.max(s, axis=0, keepdims=True))
        alpha = jnp.exp2(m_prev - m_new)
        pmat = jnp.exp2(s - m_new)
        acc_scr[...] = alpha * acc_scr[...] + jnp.dot(ct_ref[...], pmat.astype(BF16),
                                                      preferred_element_type=F32)
        m_scr[...] = m_new

    @pl.when(j < i)
    def _():
        update(False)

    @pl.when(j == i)
    def _():
        update(True)
        o_lat_t = acc_scr[0:MLA_KV_RANK, :] / acc_scr[MLA_KV_RANK:MLA_KV_RANK + 1, :]
        outs = [_mm(w_uvt_ref[h], o_lat_t[:, h * tq:(h + 1) * tq]) for h in range(MLA_H)]
        o_ref[...] = jnp.concatenate(outs, axis=0).T.astype(o_ref.dtype)


def _attn_prompt(q_t, k, c_t, w_uv_t, b, t, tq):
    nq = t // tq
    m = b * t
    assert q_t.shape == (m // tq, MLA_QK, MLA_H * tq), "query tiles must match the projection tiles"
    pairs = [(i, j) for i in range(nq) for j in range(i + 1)]
    qi = jnp.asarray(np.array([p[0] for p in pairs], np.int32))
    kj = jnp.asarray(np.array([p[1] for p in pairs], np.int32))
    grid_spec = pltpu.PrefetchScalarGridSpec(
        num_scalar_prefetch=2,
        grid=(b, len(pairs)),
        in_specs=[
            pl.BlockSpec((None, MLA_QK, MLA_H * tq), lambda bi, p, qi, kj: (bi * nq + qi[p], 0, 0)),
            pl.BlockSpec((tq, MLA_QK), lambda bi, p, qi, kj: (bi * nq + kj[p], 0)),
            pl.BlockSpec((CT_ROWS, tq), lambda bi, p, qi, kj: (0, bi * nq + kj[p])),
            pl.BlockSpec(w_uv_t.shape, lambda bi, p, qi, kj: (0, 0, 0)),
        ],
        out_specs=pl.BlockSpec((tq, MLA_H * MLA_V_D), lambda bi, p, qi, kj: (bi * nq + qi[p], 0)),
        scratch_shapes=[pltpu.VMEM((1, MLA_H * tq), F32),
                        pltpu.VMEM((CT_ROWS, MLA_H * tq), F32)],
    )
    return pl.pallas_call(
        functools.partial(_attn_prompt_kernel, tq=tq),
        grid_spec=grid_spec,
        out_shape=jax.ShapeDtypeStruct((m, MLA_H * MLA_V_D), BF16),
        compiler_params=pltpu.CompilerParams(
            dimension_semantics=("parallel", "arbitrary"), vmem_limit_bytes=VMEM_LIMIT_BYTES),
        name="attn_prompt",
    )(qi, kj, q_t, k, c_t, w_uv_t)


def _softmax_update(s, v_b, m_scr, l_scr, acc_scr):
    m_prev = m_scr[...]
    m_new = jnp.maximum(m_prev, jnp.max(s, axis=-1, keepdims=True))
    alpha = jnp.exp(m_prev - m_new)
    p = jnp.exp(s - m_new)
    l_scr[...] = alpha * l_scr[...] + jnp.sum(p, axis=-1, keepdims=True)
    acc_scr[...] = alpha * acc_scr[...] + jnp.dot(p.astype(BF16), v_b, preferred_element_type=F32)
    m_scr[...] = m_new


def _attn_sample_kernel(pt_ref, q_ref, knew_ref, *refs, pages_per_step, page, t_new):
    del pt_ref
    ckv_refs = refs[:pages_per_step]
    kpe_refs = refs[pages_per_step:2 * pages_per_step]
    w_uv_ref, o_ref, c_all, kpe_t_all, m_scr, l_scr, acc_scr = refs[2 * pages_per_step:]
    step = pl.program_id(1)
    rows = MLA_H * t_new

    @pl.when(step == 0)
    def _():
        m_scr[...] = jnp.full(m_scr.shape, -jnp.inf, F32)
        l_scr[...] = jnp.zeros(l_scr.shape, F32)
        acc_scr[...] = jnp.zeros(acc_scr.shape, F32)

    for p in range(pages_per_step):
        c_all[p * page:(p + 1) * page, :] = ckv_refs[p][...].astype(BF16)
        kpe_t_all[:, p * page:(p + 1) * page] = kpe_refs[p][...].astype(BF16)
    q = q_ref[...].reshape(rows, MLA_QK).astype(BF16)
    c_b = c_all[...]
    s = (lax.dot_general(q[:, 0:MLA_KV_RANK], c_b, _NT, preferred_element_type=F32)
         + jnp.dot(q[:, MLA_KV_RANK:MLA_QK], kpe_t_all[...], preferred_element_type=F32))
    _softmax_update(s, c_b, m_scr, l_scr, acc_scr)

    @pl.when(step == pl.num_programs(1) - 1)
    def _():
        k_new = knew_ref[...]
        s = _mm_nt(q, k_new)
        qi = lax.broadcasted_iota(jnp.int32, (rows, t_new), 0) % t_new
        ki = lax.broadcasted_iota(jnp.int32, (rows, t_new), 1)
        s = jnp.where(ki <= qi, s, -jnp.inf)
        _softmax_update(s, k_new[:, 0:MLA_KV_RANK].astype(BF16), m_scr, l_scr, acc_scr)
        o_lat = acc_scr[...] / l_scr[...]
        outs = [_mm(o_lat[h * t_new:(h + 1) * t_new], w_uv_ref[h]) for h in range(MLA_H)]
        o_ref[...] = jnp.concatenate(outs, axis=-1).astype(o_ref.dtype)


def _attn_sample(q, k_new, cache_ckv, cache_kpe, layer, page_table, w_uv, b, t_new):
    cache_kpe = jnp.swapaxes(cache_kpe, 2, 3)
    n_pages = page_table.shape[1]
    page = cache_ckv.shape[2]
    pages_per_step = math.gcd(SAMPLE_PAGES_PER_STEP, n_pages)
    rows = MLA_H * t_new
    n_steps = n_pages // pages_per_step

    def page_map(p):
        return lambda bi, s, pt: (layer, pt[bi, s * pages_per_step + p], 0, 0)

    grid_spec = pltpu.PrefetchScalarGridSpec(
        num_scalar_prefetch=1,
        grid=(b, n_steps),
        in_specs=(
            [pl.BlockSpec((MLA_H, t_new, MLA_QK), lambda bi, s, pt: (0, bi, 0)),
             pl.BlockSpec((t_new, MLA_QK), lambda bi, s, pt: (bi, 0))]
            + [pl.BlockSpec((None, None, page, MLA_KV_RANK), page_map(p)) for p in range(pages_per_step)]
            + [pl.BlockSpec((None, None, MLA_ROPE_D, page), page_map(p)) for p in range(pages_per_step)]
            + [pl.BlockSpec(w_uv.shape, lambda bi, s, pt: (0, 0, 0))]
        ),
        out_specs=pl.BlockSpec((t_new, MLA_H * MLA_V_D), lambda bi, s, pt: (bi, 0)),
        scratch_shapes=[pltpu.VMEM((pages_per_step * page, MLA_KV_RANK), BF16),
                        pltpu.VMEM((MLA_ROPE_D, pages_per_step * page), BF16),
                        pltpu.VMEM((rows, 1), F32), pltpu.VMEM((rows, 1), F32),
                        pltpu.VMEM((rows, MLA_KV_RANK), F32)],
    )
    return pl.pallas_call(
        functools.partial(_attn_sample_kernel, pages_per_step=pages_per_step, page=page, t_new=t_new),
        grid_spec=grid_spec,
        out_shape=jax.ShapeDtypeStruct((b * t_new, MLA_H * MLA_V_D), F32),
        compiler_params=pltpu.CompilerParams(
            dimension_semantics=("parallel", "arbitrary"), vmem_limit_bytes=VMEM_LIMIT_BYTES),
        name="attn_sample",
    )(page_table, q, k_new, *([cache_ckv] * pages_per_step), *([cache_kpe] * pages_per_step), w_uv)


def _ffn_kernel(x_ref, og_ref, om_ref, hist_ref, w_out_ref, ln_ref, w_up_ref, cw_ref, w_dn_ref,
                y_ref, nf_ref, *scratch, tm, alpha, short_seq):
    d_gdn = og_ref.shape[-1]
    n_chunks, _, two_fc = w_up_ref.shape
    fc = two_fc // 2
    mix = _mm(og_ref[...], w_out_ref[0:d_gdn, :]) + _mm(om_ref[...], w_out_ref[d_gdn:, :])
    x1 = _layernorm(alpha * x_ref[...] + mix, ln_ref[0:1, :], ln_ref[1:2, :])
    x1_b = x1.astype(BF16)

    if short_seq:
        nb = tm // SUBLANES
        t_pos = lax.broadcasted_iota(jnp.int32, (nb, SUBLANES, two_fc), 1)
    else:
        ubuf, carry = scratch

        @pl.when(pl.program_id(1) == 0)
        def _():
            carry[...] = hist_ref[0]

    acc = jnp.zeros((tm, x_ref.shape[-1]), F32)
    for c in range(n_chunks):
        cols = slice(c * two_fc, (c + 1) * two_fc)
        up = jnp.dot(x1_b, w_up_ref[c], preferred_element_type=F32)
        cw = cw_ref[c]
        if short_seq:
            up3 = up.reshape(nb, SUBLANES, two_fc)
            h0 = hist_ref[:, SUBLANES - 2:SUBLANES - 1, cols]
            h1 = hist_ref[:, SUBLANES - 1:SUBLANES, cols]
            prev1 = jnp.where(t_pos == 0, h1, pltpu.roll(up3, 1, axis=1))
            prev2 = jnp.where(t_pos == 0, h0, jnp.where(t_pos == 1, h1, pltpu.roll(up3, 2, axis=1)))
            hc = (prev2 * cw[0:1] + prev1 * cw[1:2] + up3 * cw[2:3]).reshape(tm, two_fc)
            nf_ref[:, :, cols] = up3
        else:
            ubuf[0:SUBLANES, :] = carry[:, cols]
            ubuf[SUBLANES:SUBLANES + tm, :] = up
            hc = (ubuf[pl.ds(SUBLANES - 2, tm), :] * cw[0:1]
                  + ubuf[pl.ds(SUBLANES - 1, tm), :] * cw[1:2] + up * cw[2:3])
            last = ubuf[tm:tm + SUBLANES, :]
            carry[:, cols] = last
            nf_ref[0, :, cols] = last
        hmid = _silu(hc[:, fc:]) * hc[:, 0:fc]
        acc = acc + _mm(hmid, w_dn_ref[c])
    y_ref[...] = _layernorm(alpha * x1 + acc, ln_ref[2:3, :], ln_ref[3:4, :])


def _ffn(x2, o_gdn, o_mla, hist8, wts, b, t, tm, alpha):
    m, d = x2.shape
    n_chunks, _, two_fc = wts["w_up"].shape
    two_ff = n_chunks * two_fc
    short_seq = t == SUBLANES
    if short_seq:
        grid = (m // tm, 1)
        nb = tm // SUBLANES
        row = lambda i, j: (i, 0)
        hist_spec = pl.BlockSpec((nb, SUBLANES, two_ff), lambda i, j: (i, 0, 0))
        scratch = []
    else:
        nt = t // tm
        grid = (b, nt)
        row = lambda i, j: (i * nt + j, 0)
        hist_spec = pl.BlockSpec((1, SUBLANES, two_ff), lambda i, j: (i, 0, 0))
        scratch = [pltpu.VMEM((tm + SUBLANES, two_fc), F32), pltpu.VMEM((SUBLANES, two_ff), F32)]
    return pl.pallas_call(
        functools.partial(_ffn_kernel, tm=tm, alpha=alpha, short_seq=short_seq),
        grid=grid,
        in_specs=[
            pl.BlockSpec((tm, d), row), pl.BlockSpec((tm, o_gdn.shape[1]), row),
            pl.BlockSpec((tm, o_mla.shape[1]), row), hist_spec,
            _const_spec(wts["w_out"].shape), _const_spec(wts["ln"].shape),
            _const_spec(wts["w_up"].shape), _const_spec(wts["ffn_cw"].shape),
            _const_spec(wts["w_dn"].shape),
        ],
        out_specs=(pl.BlockSpec((tm, d), row), hist_spec),
        out_shape=(jax.ShapeDtypeStruct((m, d), F32),
                   jax.ShapeDtypeStruct((b, SUBLANES, two_ff), F32)),
        scratch_shapes=scratch,
        compiler_params=pltpu.CompilerParams(
            dimension_semantics=("parallel", "arbitrary"), vmem_limit_bytes=VMEM_LIMIT_BYTES),
        name="ffn",
    )(x2, o_gdn, o_mla, hist8, wts["w_out"], wts["ln"], wts["w_up"], wts["ffn_cw"], wts["w_dn"])


def _swap_halves(w):
    half = w.shape[-1] // 2
    return jnp.concatenate([-w[..., half:], w[..., :half]], axis=-1)


def _prep_weights(w_in, gdn_conv_w, gdn_A_log, gdn_dt_bias, gdn_norm_w, mla_q_norm_w, mla_w_uq,
                  mla_kv_norm_w, mla_w_uk, mla_w_uv, w_out, ln1_g, ln1_b, ffn_w_up, ffn_conv_w,
                  ffn_w_down, ln2_g, ln2_b):
    d_model = w_in.shape[0]
    o = 0
    w_qkv = w_in[:, o:o + GDN_QKV]; o += GDN_QKV
    w_z = w_in[:, o:o + GDN_QK]; o += GDN_QK
    w_b = w_in[:, o:o + GDN_H]; o += GDN_H
    w_a = w_in[:, o:o + GDN_H]; o += GDN_H
    w_cq = w_in[:, o:o + MLA_Q_RANK]; o += MLA_Q_RANK
    w_ckv = w_in[:, o:o + MLA_KV_RANK]; o += MLA_KV_RANK
    w_kpe = w_in[:, o:o + MLA_ROPE_D]
    pad = jnp.zeros((d_model, LANES - 2 * GDN_H - 2 * MLA_ROPE_D), w_in.dtype)
    w_small = jnp.concatenate([w_b, w_a, w_kpe, _swap_halves(w_kpe), pad], axis=1)
    w_rest = jnp.concatenate([w_cq, w_ckv, w_small], axis=1)

    uq = mla_w_uq.reshape(MLA_Q_RANK, MLA_H, MLA_NOPE_D + MLA_ROPE_D)
    uq_nope = uq[:, :, :MLA_NOPE_D].reshape(MLA_Q_RANK, MLA_H * MLA_NOPE_D)
    uq_pe = uq[:, :, MLA_NOPE_D:]
    w_q = jnp.concatenate([uq_nope, uq_pe.reshape(MLA_Q_RANK, -1),
                           _swap_halves(uq_pe).reshape(MLA_Q_RANK, -1)], axis=1)

    gpar = jnp.zeros((2, LANES), F32)
    gpar = gpar.at[0, GDN_H:2 * GDN_H].set(gdn_A_log).at[1, GDN_H:2 * GDN_H].set(gdn_dt_bias)

    d_ff = ffn_w_down.shape[0]
    n_chunks = d_ff // FFN_CHUNK

    def interleave(w):
        lead = w.shape[:-1]
        w = w.reshape(lead + (2, n_chunks, FFN_CHUNK))
        return jnp.moveaxis(w, -3, -2).reshape(lead + (n_chunks, 2 * FFN_CHUNK))

    return {
        "w_qkv": w_qkv.astype(BF16), "w_z": w_z.astype(BF16), "w_rest": w_rest.astype(BF16),
        "w_q": w_q.astype(BF16), "w_q_t": w_q.T.astype(BF16),
        "w_uk_t": jnp.transpose(mla_w_uk, (1, 2, 0)).astype(BF16),
        "w_uk": jnp.transpose(mla_w_uk, (1, 0, 2)).astype(BF16),
        "w_uv": jnp.transpose(mla_w_uv, (1, 0, 2)).astype(BF16),
        "w_uv_t": jnp.transpose(mla_w_uv, (1, 2, 0)).astype(BF16),
        "q_norm": mla_q_norm_w.reshape(1, -1), "kv_norm": mla_kv_norm_w.reshape(1, -1),
        "gdn_conv_w": gdn_conv_w, "gpar": gpar, "gdn_norm_w": gdn_norm_w.reshape(1, -1),
        "w_out": w_out.astype(BF16),
        "ln": jnp.stack([ln1_g, ln1_b, ln2_g, ln2_b]),
        "w_up": jnp.transpose(interleave(ffn_w_up), (1, 0, 2)).astype(BF16),
        "ffn_cw": jnp.transpose(interleave(ffn_conv_w), (1, 0, 2)),
        "w_dn": ffn_w_down.reshape(n_chunks, FFN_CHUNK, -1).astype(BF16),
        "interleave": interleave, "n_chunks": n_chunks,
    }


def _rope_tables(past_len, t, tm, transposed):
    half = MLA_ROPE_D // 2
    pos = (past_len + jnp.arange(t, dtype=jnp.int32)).astype(F32)
    inv = ROPE_BASE ** (-jnp.arange(half, dtype=F32) / half)
    ang = pos[:, None] * inv
    reps = (max(tm // t, 1), 2 * MLA_H)
    cos, sin = jnp.tile(jnp.cos(ang), reps), jnp.tile(jnp.sin(ang), reps)
    if transposed:
        return cos.T, sin.T, cos[:, 0:MLA_ROPE_D], sin[:, 0:MLA_ROPE_D]
    return cos, sin


def _pad_hist(h):
    return jnp.pad(h, ((0, 0), (SUBLANES - h.shape[1], 0), (0, 0)))


def _layer(x, past, layer, s0, conv_hist, ffn_hist, wts, alpha, *, tm_proj, tm_ffn, tq, gdn_rows):
    b, t, d = x.shape
    m = b * t
    x2 = x.reshape(m, d)
    short_seq = past is not None
    past_len = past[2].shape[1] * past[0].shape[2] if short_seq else 0
    act_dtype = F32 if short_seq else BF16
    rope = _rope_tables(past_len, t, tm_proj, transposed=not short_seq)
    qkv, z, small, c_new, kpe_new, q, k, *c_t = _in_proj(x2, wts, rope, tm_proj,
                                                         transposed_q=not short_seq)

    o_gdn, s_new = _gdn(qkv.reshape(b, t, -1), z.reshape(b, t, -1), small.reshape(b, t, -1),
                        _pad_hist(conv_hist), s0, wts["gdn_conv_w"], wts["gpar"], wts["gdn_norm_w"],
                        gdn_rows, act_dtype)
    if short_seq:
        cache_ckv, cache_kpe, page_table = past
        o_mla = _attn_sample(q, k, cache_ckv, cache_kpe, layer, page_table, wts["w_uv"], b, t)
    else:
        o_mla = _attn_prompt(q, k, c_t[0], wts["w_uv_t"], b, t, tq)

    hist8 = wts["interleave"](_pad_hist(ffn_hist)).reshape(b, SUBLANES, -1)
    y, nf8 = _ffn(x2, o_gdn.reshape(m, -1), o_mla, hist8, wts, b, t, tm_ffn, alpha)

    n_chunks = wts["n_chunks"]
    nf = nf8[:, SUBLANES - (FFN_CONV_TAPS - 1):, :].reshape(b, FFN_CONV_TAPS - 1, n_chunks, 2, FFN_CHUNK)
    new_ffn = jnp.moveaxis(nf, -2, -3).reshape(b, FFN_CONV_TAPS - 1, -1)
    new_conv = qkv.reshape(b, t, -1)[:, t - (GDN_CONV_TAPS - 1):, :]
    return (y.reshape(b, t, d), c_new.reshape(b, t, -1), kpe_new.reshape(b, t, -1), s_new,
            new_conv, new_ffn)


def kernel(x_prompt, x_sample, cache_ckv, cache_kpe, page_table, state_gdn, state_gdn_conv,
           state_ffn_conv, w_in, gdn_conv_w, gdn_A_log, gdn_dt_bias, gdn_norm_w, mla_q_norm_w,
           mla_w_uq, mla_kv_norm_w, mla_w_uk, mla_w_uv, w_out, ln1_g, ln1_b, ffn_w_up, ffn_conv_w,
           ffn_w_down, ln2_g, ln2_b):
    depth = w_in.shape[0]
    alpha = (2.0 * depth) ** 0.25
    bp, tp, _ = x_prompt.shape
    bs, ts, _ = x_sample.shape
    assert ts == SUBLANES, "sample group: one 8-row tile per sequence"
    xp, xs = x_prompt, x_sample
    new_p, new_s = [], []
    for l in range(depth):
        wts = _prep_weights(w_in[l], gdn_conv_w[l], gdn_A_log[l], gdn_dt_bias[l], gdn_norm_w[l],
                            mla_q_norm_w[l], mla_w_uq[l], mla_kv_norm_w[l], mla_w_uk[l],
                            mla_w_uv[l], w_out[l], ln1_g[l], ln1_b[l], ffn_w_up[l], ffn_conv_w[l],
                            ffn_w_down[l], ln2_g[l], ln2_b[l])
        two_ff = ffn_w_up.shape[-1]
        xp, *st_p = _layer(
            xp, None, l, jnp.zeros((bp, GDN_H, GDN_D, GDN_D), F32),
            jnp.zeros((bp, GDN_CONV_TAPS - 1, GDN_QKV), F32),
            jnp.zeros((bp, FFN_CONV_TAPS - 1, two_ff), F32), wts, alpha,
            tm_proj=min(256, tp), tm_ffn=min(512, tp), tq=min(256, tp), gdn_rows=min(256, tp))
        xs, *st_s = _layer(
            xs, (cache_ckv, cache_kpe, page_table), l, state_gdn[l], state_gdn_conv[l],
            state_ffn_conv[l], wts, alpha,
            tm_proj=min(256, bs * ts), tm_ffn=min(256, bs * ts), tq=None, gdn_rows=ts)
        new_p.append(st_p)
        new_s.append(st_s)
    stack = lambda sts: [jnp.stack(v) for v in zip(*sts)]
    return (xp, xs, *stack(new_p), *stack(new_s))
```

```python
import functools
import math

import jax
import jax.numpy as jnp
import numpy as np
from jax import lax
from jax.experimental import pallas as pl
from jax.experimental.pallas import tpu as pltpu

F32 = jnp.float32
BF16 = jnp.bfloat16

GDN_H = 8
GDN_D = 64
GDN_QK = GDN_H * GDN_D
GDN_QKV = 3 * GDN_QK
GDN_CONV_TAPS = 4
GDN_CHUNK = 64
GDN_GROUP = 256
GDN_GROUPS = GDN_QK // GDN_GROUP
HEADS_PER_GROUP = GDN_GROUP // GDN_D
MLA_H = 8
MLA_Q_RANK = 256
MLA_KV_RANK = 128
MLA_NOPE_D = 64
MLA_ROPE_D = 32
MLA_V_D = 64
MLA_QK = MLA_KV_RANK + MLA_ROPE_D
MLA_SOFTMAX_SCALE = (MLA_NOPE_D + MLA_ROPE_D) ** -0.5
LOG2_E = math.log2(math.e)
CT_ROWS = MLA_KV_RANK + 16
ROPE_BASE = 10000.0
FFN_CONV_TAPS = 3
RMS_EPS = 1e-6
LN_EPS = 1e-5
L2_EPS = 1e-6

SUBLANES = 8
LANES = 128
VMEM_LIMIT_BYTES = 56 * 1024 * 1024
FFN_CHUNK = 256
SAMPLE_PAGES_PER_STEP = 32

_NT = (((1,), (1,)), ((), ()))
_TN = (((0,), (0,)), ((), ()))


def _mm(a, b):
    return jnp.dot(a.astype(BF16), b.astype(BF16), preferred_element_type=F32)


def _mm_nt(a, b):
    return lax.dot_general(a.astype(BF16), b.astype(BF16), _NT, preferred_element_type=F32)


def _sigmoid(x):
    return 1.0 / (1.0 + jnp.exp(-x))


def _silu(x):
    return x * _sigmoid(x)


def _rms(x, w):
    return x * lax.rsqrt(jnp.mean(x * x, axis=-1, keepdims=True) + RMS_EPS) * w


def _layernorm(x, g, b):
    mu = jnp.mean(x, axis=-1, keepdims=True)
    xc = x - mu
    var = jnp.mean(xc * xc, axis=-1, keepdims=True)
    return xc * lax.rsqrt(var + LN_EPS) * g + b


def _eye(n, dtype):
    return (lax.broadcasted_iota(jnp.int32, (n, n), 0)
            == lax.broadcasted_iota(jnp.int32, (n, n), 1)).astype(dtype)


def _const_spec(shape):
    nd = len(shape)
    return pl.BlockSpec(shape, lambda *_: (0,) * nd, pipeline_mode=pl.Buffered(1))


def _in_proj_kernel(x_ref, w_qkv_ref, w_z_ref, w_rest_ref, w_q_ref, w_uk_ref, qn_ref, kvn_ref,
                    *refs, transposed_q):
    if transposed_q:
        (cos_ref, sin_ref, cos_k_ref, sin_k_ref,
         qkv_ref, z_ref, small_ref, ckv_ref, kpe_ref, q_ref, k_ref, ct_ref) = refs
    else:
        cos_ref, sin_ref, qkv_ref, z_ref, small_ref, ckv_ref, kpe_ref, q_ref, k_ref = refs
    xb = x_ref[...].astype(BF16)
    qkv_ref[...] = jnp.dot(xb, w_qkv_ref[...], preferred_element_type=F32)
    z_ref[...] = jnp.dot(xb, w_z_ref[...], preferred_element_type=F32)
    rest = jnp.dot(xb, w_rest_ref[...], preferred_element_type=F32)
    cq = rest[:, 0:MLA_Q_RANK]
    ckv_raw = rest[:, MLA_Q_RANK:MLA_Q_RANK + MLA_KV_RANK]
    small = rest[:, MLA_Q_RANK + MLA_KV_RANK:]
    small_ref[...] = small

    c_new = _rms(ckv_raw, kvn_ref[...])
    ckv_ref[...] = c_new
    cqn = _rms(cq, qn_ref[...])
    n_nope = MLA_H * MLA_NOPE_D
    npe = MLA_H * MLA_ROPE_D
    k_rot_a = small[:, 16:16 + MLA_ROPE_D]
    k_rot_b = small[:, 16 + MLA_ROPE_D:16 + 2 * MLA_ROPE_D]

    if transposed_q:
        cos_t = cos_ref[...]
        sin_t = sin_ref[...]
        tm = x_ref.shape[0]
        kpe = k_rot_a * cos_k_ref[...] + k_rot_b * sin_k_ref[...]
        ct_ref[0:MLA_KV_RANK, :] = lax.dot_general(_eye(LANES, BF16), c_new.astype(BF16), _NT,
                                                   preferred_element_type=F32).astype(ct_ref.dtype)
        ct_ref[MLA_KV_RANK:, :] = jnp.ones((CT_ROWS - MLA_KV_RANK, tm), ct_ref.dtype)
        q_t = lax.dot_general(w_q_ref[...], cqn.astype(BF16), _NT,
                              preferred_element_type=F32)
        scale = MLA_SOFTMAX_SCALE * LOG2_E
        q_pe = (q_t[n_nope:n_nope + npe] * cos_t + q_t[n_nope + npe:] * sin_t) * scale
        for h in range(MLA_H):
            q_lat = _mm(w_uk_ref[h], q_t[h * MLA_NOPE_D:(h + 1) * MLA_NOPE_D]) * scale
            cols = slice(h * tm, (h + 1) * tm)
            q_ref[0:MLA_KV_RANK, cols] = q_lat.astype(q_ref.dtype)
            q_ref[MLA_KV_RANK:MLA_QK, cols] = q_pe[h * MLA_ROPE_D:(h + 1) * MLA_ROPE_D].astype(q_ref.dtype)
    else:
        cos = cos_ref[...]
        sin = sin_ref[...]
        kpe = k_rot_a * cos[:, 0:MLA_ROPE_D] + k_rot_b * sin[:, 0:MLA_ROPE_D]
        q = _mm(cqn, w_q_ref[...])
        q_pe = (q[:, n_nope:n_nope + npe] * cos + q[:, n_nope + npe:] * sin) * MLA_SOFTMAX_SCALE
        for h in range(MLA_H):
            q_lat = _mm(q[:, h * MLA_NOPE_D:(h + 1) * MLA_NOPE_D], w_uk_ref[h]) * MLA_SOFTMAX_SCALE
            q_ref[h, :, 0:MLA_KV_RANK] = q_lat.astype(q_ref.dtype)
            q_ref[h, :, MLA_KV_RANK:MLA_QK] = q_pe[:, h * MLA_ROPE_D:(h + 1) * MLA_ROPE_D].astype(q_ref.dtype)

    kpe_ref[...] = kpe
    k_ref[:, 0:MLA_KV_RANK] = c_new.astype(k_ref.dtype)
    k_ref[:, MLA_KV_RANK:MLA_QK] = kpe.astype(k_ref.dtype)


def _in_proj(x2, wts, rope, tm, transposed_q):
    m, d = x2.shape
    row = lambda i: (i, 0)
    npe = MLA_H * MLA_ROPE_D
    act_dtype = BF16 if transposed_q else F32
    out_shape = [
        jax.ShapeDtypeStruct((m, GDN_QKV), F32),
        jax.ShapeDtypeStruct((m, GDN_QK), F32),
        jax.ShapeDtypeStruct((m, LANES), F32),
        jax.ShapeDtypeStruct((m, MLA_KV_RANK), F32),
        jax.ShapeDtypeStruct((m, MLA_ROPE_D), F32),
    ]
    out_specs = [
        pl.BlockSpec((tm, GDN_QKV), row), pl.BlockSpec((tm, GDN_QK), row),
        pl.BlockSpec((tm, LANES), row), pl.BlockSpec((tm, MLA_KV_RANK), row),
        pl.BlockSpec((tm, MLA_ROPE_D), row),
    ]
    if transposed_q:
        n_rope_tiles = rope[0].shape[1] // tm
        rope_specs = [pl.BlockSpec((npe, tm), lambda i: (0, i % n_rope_tiles))] * 2
        rope_specs += [pl.BlockSpec((tm, MLA_ROPE_D), lambda i: (i % n_rope_tiles, 0))] * 2
        w_q, w_uk = wts["w_q_t"], wts["w_uk"]
        out_shape += [jax.ShapeDtypeStruct((m // tm, MLA_QK, MLA_H * tm), act_dtype),
                      jax.ShapeDtypeStruct((m, MLA_QK), act_dtype),
                      jax.ShapeDtypeStruct((CT_ROWS, m), act_dtype)]
        out_specs += [pl.BlockSpec((None, MLA_QK, MLA_H * tm), lambda i: (i, 0, 0)),
                      pl.BlockSpec((tm, MLA_QK), row),
                      pl.BlockSpec((CT_ROWS, tm), lambda i: (0, i))]
    else:
        n_rope_tiles = rope[0].shape[0] // tm
        rope_specs = [pl.BlockSpec((tm, npe), lambda i: (i % n_rope_tiles, 0))] * 2
        w_q, w_uk = wts["w_q"], wts["w_uk_t"]
        out_shape += [jax.ShapeDtypeStruct((MLA_H, m, MLA_QK), act_dtype),
                      jax.ShapeDtypeStruct((m, MLA_QK), act_dtype)]
        out_specs += [pl.BlockSpec((MLA_H, tm, MLA_QK), lambda i: (0, i, 0)),
                      pl.BlockSpec((tm, MLA_QK), row)]
    return pl.pallas_call(
        functools.partial(_in_proj_kernel, transposed_q=transposed_q),
        grid=(m // tm,),
        in_specs=[
            pl.BlockSpec((tm, d), row),
            _const_spec(wts["w_qkv"].shape), _const_spec(wts["w_z"].shape),
            _const_spec(wts["w_rest"].shape), _const_spec(w_q.shape), _const_spec(w_uk.shape),
            _const_spec(wts["q_norm"].shape), _const_spec(wts["kv_norm"].shape),
            *rope_specs,
        ],
        out_specs=tuple(out_specs),
        out_shape=tuple(out_shape),
        compiler_params=pltpu.CompilerParams(
            dimension_semantics=("parallel",), vmem_limit_bytes=VMEM_LIMIT_BYTES),
        name="in_proj",
    )(x2, wts["w_qkv"], wts["w_z"], wts["w_rest"], w_q, w_uk, wts["q_norm"], wts["kv_norm"],
      *rope)


def _gdn_kernel(qkv_ref, z_ref, small_ref, hist_ref, s0_ref, convw_ref, gpar_ref, normw_ref,
                o_ref, sout_ref, xbuf, s_scr, o_buf, *, t_in, tt):
    t = pl.program_id(1)
    c_rows, gw = GDN_CHUNK, GDN_GROUP
    padded = tt > t_in

    @pl.when(t == 0)
    def _():
        xbuf[0:SUBLANES, :] = hist_ref[0]
        s_scr[...] = jnp.zeros(s_scr.shape, F32)
        for h in range(GDN_H):
            g, hh = divmod(h, HEADS_PER_GROUP)
            s_scr[g, hh * GDN_D:(hh + 1) * GDN_D, hh * GDN_D:(hh + 1) * GDN_D] = s0_ref[0, h]

    if padded:
        xbuf[SUBLANES + t_in:SUBLANES + tt, :] = jnp.zeros((tt - t_in, GDN_QKV), F32)
    xbuf[SUBLANES:SUBLANES + t_in, :] = qkv_ref[0]
    cw = convw_ref[...]
    y = xbuf[pl.ds(SUBLANES - 3, tt), :] * cw[0:1]
    for j in range(1, GDN_CONV_TAPS):
        y = y + xbuf[pl.ds(SUBLANES - 3 + j, tt), :] * cw[j:j + 1]
    qkv = _silu(y)
    xbuf[0:SUBLANES, :] = xbuf[t_in:t_in + SUBLANES, :]

    small = small_ref[0]
    z = z_ref[0]
    if padded:
        row_valid = lax.broadcasted_iota(jnp.int32, (tt, 1), 0) < t_in
        qkv = jnp.where(row_valid, qkv, 0.0)
        small = jnp.concatenate([small, jnp.zeros((tt - t_in, LANES), F32)], axis=0)
        z = jnp.concatenate([z, jnp.zeros((tt - t_in, GDN_QK), F32)], axis=0)
    beta_all = _sigmoid(small)
    xa = small + gpar_ref[1:2, :]
    softplus = jnp.maximum(xa, 0.0) + jnp.log1p(jnp.exp(-jnp.abs(xa)))
    g_all = -jnp.exp(gpar_ref[0:1, :]) * softplus
    if padded:
        beta_all = jnp.where(row_valid, beta_all, 0.0)
        g_all = jnp.where(row_valid, g_all, 0.0)

    def split3(x):
        hi = x.astype(BF16)
        r1 = x - hi.astype(F32)
        mid = r1.astype(BF16)
        return hi, mid, (r1 - mid.astype(F32)).astype(BF16)

    def mm_exact(x, ones_mat, terms=3):
        return sum(jnp.dot(part, ones_mat, preferred_element_type=F32) for part in split3(x)[:terms])

    ri = lax.broadcasted_iota(jnp.int32, (tt, tt), 0)
    ci = lax.broadcasted_iota(jnp.int32, (tt, tt), 1)
    tril_blocks = ((ri >= ci) & (ri // c_rows == ci // c_rows)).astype(BF16)
    gc_all = sum(jnp.dot(tril_blocks, part, preferred_element_type=F32) for part in split3(g_all))
    lane = lax.broadcasted_iota(jnp.int32, (LANES, GDN_QK), 0)
    head = lax.broadcasted_iota(jnp.int32, (LANES, GDN_QK), 1) // GDN_D
    beta_x = mm_exact(beta_all, (lane == head).astype(BF16))
    gc_x = mm_exact(gc_all, (lane == head + GDN_H).astype(BF16))

    br = lax.broadcasted_iota(jnp.int32, (gw, gw), 0) // GDN_D
    bc = lax.broadcasted_iota(jnp.int32, (gw, gw), 1) // GDN_D
    block_mask = br == bc
    block_ones = block_mask.astype(BF16)

    def block_diag(x):
        return jnp.concatenate([x.astype(BF16)] * HEADS_PER_GROUP, axis=0) * block_ones

    rr = lax.broadcasted_iota(jnp.int32, (c_rows, gw), 0)
    cc = lax.broadcasted_iota(jnp.int32, (c_rows, gw), 1) % c_rows
    incl, strict, diag = rr >= cc, rr > cc, rr == cc
    eye_t = diag.astype(F32)
    n_levels = int(math.log2(c_rows)) - 1
    n_chunks = tt // c_rows

    q_n, k_n = [], []
    for g in range(GDN_GROUPS):
        q_g = qkv[:, g * gw:(g + 1) * gw]
        k_g = qkv[:, GDN_QK + g * gw:GDN_QK + (g + 1) * gw]
        q_n.append(q_g * lax.rsqrt(mm_exact(q_g * q_g, block_ones, 2) + L2_EPS) * (GDN_D ** -0.5))
        k_n.append(k_g * lax.rsqrt(mm_exact(k_g * k_g, block_ones, 2) + L2_EPS))

    units = [(g, c) for c in range(n_chunks) for g in range(GDN_GROUPS)]
    st = {}
    for g, c in units:
        rows = slice(c * c_rows, (c + 1) * c_rows)
        lanes = slice(g * gw, (g + 1) * gw)
        q_c, k_c = q_n[g][rows], k_n[g][rows]
        v_c = qkv[rows, 2 * GDN_QK + g * gw:2 * GDN_QK + (g + 1) * gw]
        b_c = beta_x[rows, lanes]
        gc_c = gc_x[rows, lanes]
        g_row = jnp.sum(jnp.where(diag, gc_c, 0.0), axis=0, keepdims=True)
        g_last = gc_c[c_rows - 1:c_rows, :]
        decay = jnp.exp(jnp.where(incl, gc_c - g_row, -jnp.inf))
        e_col = jnp.exp(gc_c)
        kb = k_c * b_c
        kk_qk = lax.dot_general(jnp.concatenate([kb, q_c], axis=0).astype(BF16), block_diag(k_c),
                                _NT, preferred_element_type=F32)
        lmat = jnp.where(strict, kk_qk[0:c_rows] * decay, 0.0)
        st[g, c] = dict(
            inv=eye_t - lmat, lmat=lmat,
            a=(kk_qk[c_rows:] * decay).astype(BF16),
            vb=block_diag(v_c * b_c), kbe=block_diag(kb * e_col),
            qg=q_c * e_col,
            kg=(k_c * jnp.exp(g_last - gc_c)).astype(BF16),
            s_decay=jnp.exp(g_last),
        )
    for u in units:
        st[u]["pw"] = _mm(st[u]["lmat"], block_diag(st[u]["lmat"]))
    for lvl in range(n_levels):
        for u in units:
            s_u = st[u]
            bd = block_diag(s_u["pw"])
            if lvl < n_levels - 1:
                both = _mm(jnp.concatenate([s_u["pw"], s_u["inv"]], axis=0), bd)
                s_u["pw"] = both[0:c_rows]
                s_u["inv"] = s_u["inv"] + both[c_rows:]
            else:
                s_u["inv"] = s_u["inv"] + _mm(s_u["inv"], bd)
    for u in units:
        inv_b = st[u]["inv"].astype(BF16)
        st[u]["u"] = jnp.dot(inv_b, st[u]["vb"], preferred_element_type=F32)
        st[u]["w"] = jnp.dot(inv_b, st[u]["kbe"], preferred_element_type=F32)

    s_bd = [s_scr[g] for g in range(GDN_GROUPS)]
    for g, c in units:
        p = st[g, c]
        both = _mm(jnp.concatenate([p["w"], p["qg"]], axis=0), s_bd[g])
        v_new = p["u"] - both[0:c_rows]
        o_buf[c * c_rows:(c + 1) * c_rows, g * gw:(g + 1) * gw] = (
            both[c_rows:] + jnp.dot(p["a"], block_diag(v_new), preferred_element_type=F32))
        upd = lax.dot_general(p["kg"], v_new.astype(BF16), _TN, preferred_element_type=F32)
        s_bd[g] = s_bd[g] * p["s_decay"] + jnp.where(block_mask, upd, 0.0)
    for g in range(GDN_GROUPS):
        s_scr[g] = s_bd[g]

    norm_w = jnp.concatenate([normw_ref[...]] * HEADS_PER_GROUP, axis=1)
    outs = []
    for g in range(GDN_GROUPS):
        o_g = o_buf[:, g * gw:(g + 1) * gw]
        ms = mm_exact(o_g * o_g, block_ones, 2) * (1.0 / GDN_D)
        outs.append(o_g * lax.rsqrt(ms + RMS_EPS) * norm_w * _silu(z[:, g * gw:(g + 1) * gw]))
    o_ref[0] = jnp.concatenate(outs, axis=1)[0:t_in].astype(o_ref.dtype)

    @pl.when(t == pl.num_programs(1) - 1)
    def _():
        for h in range(GDN_H):
            g, hh = divmod(h, HEADS_PER_GROUP)
            sout_ref[0, h] = s_scr[g, hh * GDN_D:(hh + 1) * GDN_D, hh * GDN_D:(hh + 1) * GDN_D]


def _gdn(qkv, z, small, hist8, s0, conv_w, gpar, norm_w, t_in, act_dtype):
    b, t, _ = qkv.shape
    tt = max(t_in, GDN_CHUNK)
    seq = lambda i, j: (i, j, 0)
    per_b3 = lambda i, j: (i, 0, 0)
    per_b4 = lambda i, j: (i, 0, 0, 0)
    return pl.pallas_call(
        functools.partial(_gdn_kernel, t_in=t_in, tt=tt),
        grid=(b, t // t_in),
        in_specs=[
            pl.BlockSpec((1, t_in, GDN_QKV), seq), pl.BlockSpec((1, t_in, GDN_QK), seq),
            pl.BlockSpec((1, t_in, LANES), seq),
            pl.BlockSpec((1, SUBLANES, GDN_QKV), per_b3),
            pl.BlockSpec((1, GDN_H, GDN_D, GDN_D), per_b4),
            _const_spec(conv_w.shape), _const_spec(gpar.shape), _const_spec(norm_w.shape),
        ],
        out_specs=(pl.BlockSpec((1, t_in, GDN_QK), seq),
                   pl.BlockSpec((1, GDN_H, GDN_D, GDN_D), per_b4)),
        out_shape=(jax.ShapeDtypeStruct((b, t, GDN_QK), act_dtype),
                   jax.ShapeDtypeStruct((b, GDN_H, GDN_D, GDN_D), F32)),
        scratch_shapes=[pltpu.VMEM((tt + SUBLANES, GDN_QKV), F32),
                        pltpu.VMEM((GDN_GROUPS, GDN_GROUP, GDN_GROUP), F32),
                        pltpu.VMEM((tt, GDN_QK), F32)],
        compiler_params=pltpu.CompilerParams(
            dimension_semantics=("parallel", "arbitrary"), vmem_limit_bytes=VMEM_LIMIT_BYTES),
        name="gdn",
    )(qkv, z, small, hist8, s0, conv_w, gpar, norm_w)


def _attn_prompt_kernel(qi_ref, kj_ref, qt_ref, k_ref, ct_ref, w_uvt_ref, o_ref, m_scr, acc_scr,
                        *, tq, tk):
    p = pl.program_id(1)
    i = qi_ref[p]
    j = kj_ref[p]
    kv_per_q = tq // tk
    first_diag = i * kv_per_q

    @pl.when(j == 0)
    def _():
        m_scr[...] = jnp.full(m_scr.shape, -jnp.inf, F32)
        acc_scr[...] = jnp.zeros(acc_scr.shape, F32)

    def update(masked):
        s = jnp.dot(k_ref[...], qt_ref[...], preferred_element_type=F32)
        if masked:
            key = lax.broadcasted_iota(jnp.int32, s.shape, 0) + (j - first_diag) * tk
            qry = lax.broadcasted_iota(jnp.int32, s.shape, 1) % tq
            s = jnp.where(key <= qry, s, -jnp.inf)
        m_prev = m_scr[...]
        m_new = jnp.maximum(m_prev, jnp.max(s, axis=0, keepdims=True))
        alpha = jnp.exp2(m_prev - m_new)
        pmat = jnp.exp2(s - m_new)
        acc_scr[...] = alpha * acc_scr[...] + jnp.dot(ct_ref[...], pmat.astype(BF16),
                                                      preferred_element_type=F32)
        m_scr[...] = m_new

    @pl.when(j < first_diag)
    def _():
        update(False)

    @pl.when(j >= first_diag)
    def _():
        update(True)

    @pl.when(j == first_diag + kv_per_q - 1)
    def _():
        o_lat_t = acc_scr[0:MLA_KV_RANK, :] / acc_scr[MLA_KV_RANK:MLA_KV_RANK + 1, :]
        outs = [_mm(w_uvt_ref[h], o_lat_t[:, h * tq:(h + 1) * tq]) for h in range(MLA_H)]
        o_ref[...] = jnp.concatenate(outs, axis=0).T.astype(o_ref.dtype)


def _attn_prompt(q_t, k, c_t, w_uv_t, b, t, tq, tk):
    nq, nk = t // tq, t // tk
    kv_per_q = tq // tk
    m = b * t
    assert q_t.shape == (m // tq, MLA_QK, MLA_H * tq), "query tiles must match the projection tiles"
    pairs = [(i, j) for i in range(nq) for j in range((i + 1) * kv_per_q)]
    qi = jnp.asarray(np.array([p[0] for p in pairs], np.int32))
    kj = jnp.asarray(np.array([p[1] for p in pairs], np.int32))
    grid_spec = pltpu.PrefetchScalarGridSpec(
        num_scalar_prefetch=2,
        grid=(b, len(pairs)),
        in_specs=[
            pl.BlockSpec((None, MLA_QK, MLA_H * tq), lambda bi, p, qi, kj: (bi * nq + qi[p], 0, 0)),
            pl.BlockSpec((tk, MLA_QK), lambda bi, p, qi, kj: (bi * nk + kj[p], 0)),
            pl.BlockSpec((CT_ROWS, tk), lambda bi, p, qi, kj: (0, bi * nk + kj[p])),
            pl.BlockSpec(w_uv_t.shape, lambda bi, p, qi, kj: (0, 0, 0)),
        ],
        out_specs=pl.BlockSpec((tq, MLA_H * MLA_V_D), lambda bi, p, qi, kj: (bi * nq + qi[p], 0)),
        scratch_shapes=[pltpu.VMEM((1, MLA_H * tq), F32),
                        pltpu.VMEM((CT_ROWS, MLA_H * tq), F32)],
    )
    return pl.pallas_call(
        functools.partial(_attn_prompt_kernel, tq=tq, tk=tk),
        grid_spec=grid_spec,
        out_shape=jax.ShapeDtypeStruct((m, MLA_H * MLA_V_D), BF16),
        compiler_params=pltpu.CompilerParams(
            dimension_semantics=("parallel", "arbitrary"), vmem_limit_bytes=VMEM_LIMIT_BYTES),
        name="attn_prompt",
    )(qi, kj, q_t, k, c_t, w_uv_t)


def _softmax_update(s, v_b, m_scr, l_scr, acc_scr):
    m_prev = m_scr[...]
    m_new = jnp.maximum(m_prev, jnp.max(s, axis=-1, keepdims=True))
    alpha = jnp.exp(m_prev - m_new)
    p = jnp.exp(s - m_new)
    l_scr[...] = alpha * l_scr[...] + jnp.sum(p, axis=-1, keepdims=True)
    acc_scr[...] = alpha * acc_scr[...] + jnp.dot(p.astype(BF16), v_b, preferred_element_type=F32)
    m_scr[...] = m_new


def _attn_sample_kernel(pt_ref, q_ref, knew_ref, w_uv_ref, ckv_hbm, kpe_hbm, o_ref,
                        c_buf, kpe_buf, sems, m_scr, l_scr, acc_scr, *,
                        layer, pages_per_step, page, t_new):
    seq = pl.program_id(0)
    grp = pl.program_id(1)
    n_grp = pl.num_programs(1)
    step = seq * n_grp + grp
    slot = step % 2
    rows = MLA_H * t_new

    def page_copies(seq_i, grp_i, slot_i, p):
        pid = pt_ref[seq_i, grp_i * pages_per_step + p]
        return (
            pltpu.make_async_copy(ckv_hbm.at[layer, pid],
                                  c_buf.at[slot_i, pl.ds(p * page, page), :], sems.at[0, slot_i]),
            pltpu.make_async_copy(kpe_hbm.at[layer, pid],
                                  kpe_buf.at[slot_i, :, pl.ds(p * page, page)], sems.at[1, slot_i]),
        )

    def start_fetch(seq_i, grp_i, slot_i):
        for p in range(pages_per_step):
            for cp in page_copies(seq_i, grp_i, slot_i, p):
                cp.start()

    @pl.when(step == 0)
    def _():
        start_fetch(seq, grp, slot)

    @pl.when(step + 1 < pl.num_programs(0) * n_grp)
    def _():
        nxt = step + 1
        start_fetch(nxt // n_grp, nxt % n_grp, 1 - slot)

    @pl.when(grp == 0)
    def _():
        m_scr[...] = jnp.full(m_scr.shape, -jnp.inf, F32)
        l_scr[...] = jnp.zeros(l_scr.shape, F32)
        acc_scr[...] = jnp.zeros(acc_scr.shape, F32)

    for p in range(pages_per_step):
        for cp in page_copies(seq, grp, slot, p):
            cp.wait()

    q = q_ref[...].reshape(rows, MLA_QK).astype(BF16)
    c_b = c_buf[slot].astype(BF16)
    s = (lax.dot_general(q[:, 0:MLA_KV_RANK], c_b, _NT, preferred_element_type=F32)
         + jnp.dot(q[:, MLA_KV_RANK:MLA_QK], kpe_buf[slot].astype(BF16), preferred_element_type=F32))
    _softmax_update(s, c_b, m_scr, l_scr, acc_scr)

    @pl.when(grp == n_grp - 1)
    def _():
        k_new = knew_ref[...]
        s = _mm_nt(q, k_new)
        qi = lax.broadcasted_iota(jnp.int32, (rows, t_new), 0) % t_new
        ki = lax.broadcasted_iota(jnp.int32, (rows, t_new), 1)
        s = jnp.where(ki <= qi, s, -jnp.inf)
        _softmax_update(s, k_new[:, 0:MLA_KV_RANK].astype(BF16), m_scr, l_scr, acc_scr)
        o_lat = acc_scr[...] / l_scr[...]
        outs = [_mm(o_lat[h * t_new:(h + 1) * t_new], w_uv_ref[h]) for h in range(MLA_H)]
        o_ref[...] = jnp.concatenate(outs, axis=-1).astype(o_ref.dtype)


def _attn_sample(q, k_new, cache_ckv, cache_kpe, layer, page_table, w_uv, b, t_new):
    cache_kpe = jnp.swapaxes(cache_kpe, 2, 3)
    n_pages = page_table.shape[1]
    page = cache_ckv.shape[2]
    pages_per_step = math.gcd(SAMPLE_PAGES_PER_STEP, n_pages)
    rows = MLA_H * t_new
    n_steps = n_pages // pages_per_step

    keys = pages_per_step * page
    grid_spec = pltpu.PrefetchScalarGridSpec(
        num_scalar_prefetch=1,
        grid=(b, n_steps),
        in_specs=[
            pl.BlockSpec((MLA_H, t_new, MLA_QK), lambda bi, s, pt: (0, bi, 0)),
            pl.BlockSpec((t_new, MLA_QK), lambda bi, s, pt: (bi, 0)),
            pl.BlockSpec(w_uv.shape, lambda bi, s, pt: (0, 0, 0)),
            pl.BlockSpec(memory_space=pl.ANY),
            pl.BlockSpec(memory_space=pl.ANY),
        ],
        out_specs=pl.BlockSpec((t_new, MLA_H * MLA_V_D), lambda bi, s, pt: (bi, 0)),
        scratch_shapes=[pltpu.VMEM((2, keys, MLA_KV_RANK), F32),
                        pltpu.VMEM((2, MLA_ROPE_D, keys), F32),
                        pltpu.SemaphoreType.DMA((2, 2)),
                        pltpu.VMEM((rows, 1), F32), pltpu.VMEM((rows, 1), F32),
                        pltpu.VMEM((rows, MLA_KV_RANK), F32)],
    )
    return pl.pallas_call(
        functools.partial(_attn_sample_kernel, layer=layer, pages_per_step=pages_per_step,
                          page=page, t_new=t_new),
        grid_spec=grid_spec,
        out_shape=jax.ShapeDtypeStruct((b * t_new, MLA_H * MLA_V_D), F32),
        compiler_params=pltpu.CompilerParams(
            dimension_semantics=("arbitrary", "arbitrary"), vmem_limit_bytes=VMEM_LIMIT_BYTES),
        name="attn_sample",
    )(page_table, q, k_new, w_uv, cache_ckv, cache_kpe)


def _ffn_kernel(x_ref, og_ref, om_ref, hist_ref, w_out_ref, ln_ref, w_up_ref, cw_ref, w_dn_ref,
                y_ref, nf_ref, *scratch, tm, alpha, short_seq):
    d_gdn = og_ref.shape[-1]
    n_chunks, _, two_fc = w_up_ref.shape
    fc = two_fc // 2
    mix = _mm(og_ref[...], w_out_ref[0:d_gdn, :]) + _mm(om_ref[...], w_out_ref[d_gdn:, :])
    x1 = _layernorm(alpha * x_ref[...] + mix, ln_ref[0:1, :], ln_ref[1:2, :])
    x1_b = x1.astype(BF16)

    if short_seq:
        nb = tm // SUBLANES
        t_pos = lax.broadcasted_iota(jnp.int32, (nb, SUBLANES, two_fc), 1)
    else:
        ubuf, carry = scratch

        @pl.when(pl.program_id(1) == 0)
        def _():
            carry[...] = hist_ref[0]

    acc = jnp.zeros((tm, x_ref.shape[-1]), F32)
    for c in range(n_chunks):
        cols = slice(c * two_fc, (c + 1) * two_fc)
        up = jnp.dot(x1_b, w_up_ref[c], preferred_element_type=F32)
        cw = cw_ref[c]
        if short_seq:
            up3 = up.reshape(nb, SUBLANES, two_fc)
            h0 = hist_ref[:, SUBLANES - 2:SUBLANES - 1, cols]
            h1 = hist_ref[:, SUBLANES - 1:SUBLANES, cols]
            prev1 = jnp.where(t_pos == 0, h1, pltpu.roll(up3, 1, axis=1))
            prev2 = jnp.where(t_pos == 0, h0, jnp.where(t_pos == 1, h1, pltpu.roll(up3, 2, axis=1)))
            hc = (prev2 * cw[0:1] + prev1 * cw[1:2] + up3 * cw[2:3]).reshape(tm, two_fc)
            nf_ref[:, :, cols] = up3
        else:
            ubuf[0:SUBLANES, :] = carry[:, cols]
            ubuf[SUBLANES:SUBLANES + tm, :] = up
            hc = (ubuf[pl.ds(SUBLANES - 2, tm), :] * cw[0:1]
                  + ubuf[pl.ds(SUBLANES - 1, tm), :] * cw[1:2] + up * cw[2:3])
            last = ubuf[tm:tm + SUBLANES, :]
            carry[:, cols] = last
            nf_ref[0, :, cols] = last
        hmid = _silu(hc[:, fc:]) * hc[:, 0:fc]
        acc = acc + _mm(hmid, w_dn_ref[c])
    y_ref[...] = _layernorm(alpha * x1 + acc, ln_ref[2:3, :], ln_ref[3:4, :])


def _ffn(x2, o_gdn, o_mla, hist8, wts, b, t, tm, alpha):
    m, d = x2.shape
    n_chunks, _, two_fc = wts["w_up"].shape
    two_ff = n_chunks * two_fc
    short_seq = t == SUBLANES
    if short_seq:
        grid = (m // tm, 1)
        nb = tm // SUBLANES
        row = lambda i, j: (i, 0)
        hist_spec = pl.BlockSpec((nb, SUBLANES, two_ff), lambda i, j: (i, 0, 0))
        scratch = []
    else:
        nt = t // tm
        grid = (b, nt)
        row = lambda i, j: (i * nt + j, 0)
        hist_spec = pl.BlockSpec((1, SUBLANES, two_ff), lambda i, j: (i, 0, 0))
        scratch = [pltpu.VMEM((tm + SUBLANES, two_fc), F32), pltpu.VMEM((SUBLANES, two_ff), F32)]
    return pl.pallas_call(
        functools.partial(_ffn_kernel, tm=tm, alpha=alpha, short_seq=short_seq),
        grid=grid,
        in_specs=[
            pl.BlockSpec((tm, d), row), pl.BlockSpec((tm, o_gdn.shape[1]), row),
            pl.BlockSpec((tm, o_mla.shape[1]), row), hist_spec,
            _const_spec(wts["w_out"].shape), _const_spec(wts["ln"].shape),
            _const_spec(wts["w_up"].shape), _const_spec(wts["ffn_cw"].shape),
            _const_spec(wts["w_dn"].shape),
        ],
        out_specs=(pl.BlockSpec((tm, d), row), hist_spec),
        out_shape=(jax.ShapeDtypeStruct((m, d), F32),
                   jax.ShapeDtypeStruct((b, SUBLANES, two_ff), F32)),
        scratch_shapes=scratch,
        compiler_params=pltpu.CompilerParams(
            dimension_semantics=("parallel", "arbitrary"), vmem_limit_bytes=VMEM_LIMIT_BYTES),
        name="ffn",
    )(x2, o_gdn, o_mla, hist8, wts["w_out"], wts["ln"], wts["w_up"], wts["ffn_cw"], wts["w_dn"])


def _swap_halves(w):
    half = w.shape[-1] // 2
    return jnp.concatenate([-w[..., half:], w[..., :half]], axis=-1)


def _prep_weights(w_in, gdn_conv_w, gdn_A_log, gdn_dt_bias, gdn_norm_w, mla_q_norm_w, mla_w_uq,
                  mla_kv_norm_w, mla_w_uk, mla_w_uv, w_out, ln1_g, ln1_b, ffn_w_up, ffn_conv_w,
                  ffn_w_down, ln2_g, ln2_b):
    d_model = w_in.shape[0]
    o = 0
    w_qkv = w_in[:, o:o + GDN_QKV]; o += GDN_QKV
    w_z = w_in[:, o:o + GDN_QK]; o += GDN_QK
    w_b = w_in[:, o:o + GDN_H]; o += GDN_H
    w_a = w_in[:, o:o + GDN_H]; o += GDN_H
    w_cq = w_in[:, o:o + MLA_Q_RANK]; o += MLA_Q_RANK
    w_ckv = w_in[:, o:o + MLA_KV_RANK]; o += MLA_KV_RANK
    w_kpe = w_in[:, o:o + MLA_ROPE_D]
    pad = jnp.zeros((d_model, LANES - 2 * GDN_H - 2 * MLA_ROPE_D), w_in.dtype)
    w_small = jnp.concatenate([w_b, w_a, w_kpe, _swap_halves(w_kpe), pad], axis=1)
    w_rest = jnp.concatenate([w_cq, w_ckv, w_small], axis=1)

    uq = mla_w_uq.reshape(MLA_Q_RANK, MLA_H, MLA_NOPE_D + MLA_ROPE_D)
    uq_nope = uq[:, :, :MLA_NOPE_D].reshape(MLA_Q_RANK, MLA_H * MLA_NOPE_D)
    uq_pe = uq[:, :, MLA_NOPE_D:]
    w_q = jnp.concatenate([uq_nope, uq_pe.reshape(MLA_Q_RANK, -1),
                           _swap_halves(uq_pe).reshape(MLA_Q_RANK, -1)], axis=1)

    gpar = jnp.zeros((2, LANES), F32)
    gpar = gpar.at[0, GDN_H:2 * GDN_H].set(gdn_A_log).at[1, GDN_H:2 * GDN_H].set(gdn_dt_bias)

    d_ff = ffn_w_down.shape[0]
    n_chunks = d_ff // FFN_CHUNK

    def interleave(w):
        lead = w.shape[:-1]
        w = w.reshape(lead + (2, n_chunks, FFN_CHUNK))
        return jnp.moveaxis(w, -3, -2).reshape(lead + (n_chunks, 2 * FFN_CHUNK))

    return {
        "w_qkv": w_qkv.astype(BF16), "w_z": w_z.astype(BF16), "w_rest": w_rest.astype(BF16),
        "w_q": w_q.astype(BF16), "w_q_t": w_q.T.astype(BF16),
        "w_uk_t": jnp.transpose(mla_w_uk, (1, 2, 0)).astype(BF16),
        "w_uk": jnp.transpose(mla_w_uk, (1, 0, 2)).astype(BF16),
        "w_uv": jnp.transpose(mla_w_uv, (1, 0, 2)).astype(BF16),
        "w_uv_t": jnp.transpose(mla_w_uv, (1, 2, 0)).astype(BF16),
        "q_norm": mla_q_norm_w.reshape(1, -1), "kv_norm": mla_kv_norm_w.reshape(1, -1),
        "gdn_conv_w": gdn_conv_w, "gpar": gpar, "gdn_norm_w": gdn_norm_w.reshape(1, -1),
        "w_out": w_out.astype(BF16),
        "ln": jnp.stack([ln1_g, ln1_b, ln2_g, ln2_b]),
        "w_up": jnp.transpose(interleave(ffn_w_up), (1, 0, 2)).astype(BF16),
        "ffn_cw": jnp.transpose(interleave(ffn_conv_w), (1, 0, 2)),
        "w_dn": ffn_w_down.reshape(n_chunks, FFN_CHUNK, -1).astype(BF16),
        "interleave": interleave, "n_chunks": n_chunks,
    }


def _rope_tables(past_len, t, tm, transposed):
    half = MLA_ROPE_D // 2
    pos = (past_len + jnp.arange(t, dtype=jnp.int32)).astype(F32)
    inv = ROPE_BASE ** (-jnp.arange(half, dtype=F32) / half)
    ang = pos[:, None] * inv
    reps = (max(tm // t, 1), 2 * MLA_H)
    cos, sin = jnp.tile(jnp.cos(ang), reps), jnp.tile(jnp.sin(ang), reps)
    if transposed:
        return cos.T, sin.T, cos[:, 0:MLA_ROPE_D], sin[:, 0:MLA_ROPE_D]
    return cos, sin


def _pad_hist(h):
    return jnp.pad(h, ((0, 0), (SUBLANES - h.shape[1], 0), (0, 0)))


def _tiles(b, t):
    if t == SUBLANES:
        rows = min(256, b * t)
        return dict(proj=rows, ffn=rows, gdn=t, attn_q=None, attn_k=None)
    return dict(proj=min(512, t), ffn=min(512, t), gdn=min(256, t), attn_q=min(512, t),
                attn_k=min(256, t))


def _layer(x, past, layer, s0, conv_hist, ffn_hist, wts, alpha):
    b, t, d = x.shape
    m = b * t
    x2 = x.reshape(m, d)
    tiles = _tiles(b, t)
    short_seq = past is not None
    past_len = past[2].shape[1] * past[0].shape[2] if short_seq else 0
    act_dtype = F32 if short_seq else BF16
    rope = _rope_tables(past_len, t, tiles["proj"], transposed=not short_seq)
    qkv, z, small, c_new, kpe_new, q, k, *c_t = _in_proj(x2, wts, rope, tiles["proj"],
                                                         transposed_q=not short_seq)

    o_gdn, s_new = _gdn(qkv.reshape(b, t, -1), z.reshape(b, t, -1), small.reshape(b, t, -1),
                        _pad_hist(conv_hist), s0, wts["gdn_conv_w"], wts["gpar"], wts["gdn_norm_w"],
                        tiles["gdn"], act_dtype)
    if short_seq:
        cache_ckv, cache_kpe, page_table = past
        o_mla = _attn_sample(q, k, cache_ckv, cache_kpe, layer, page_table, wts["w_uv"], b, t)
    else:
        o_mla = _attn_prompt(q, k, c_t[0], wts["w_uv_t"], b, t, tiles["attn_q"], tiles["attn_k"])

    hist8 = wts["interleave"](_pad_hist(ffn_hist)).reshape(b, SUBLANES, -1)
    y, nf8 = _ffn(x2, o_gdn.reshape(m, -1), o_mla, hist8, wts, b, t, tiles["ffn"], alpha)

    n_chunks = wts["n_chunks"]
    nf = nf8[:, SUBLANES - (FFN_CONV_TAPS - 1):, :].reshape(b, FFN_CONV_TAPS - 1, n_chunks, 2, FFN_CHUNK)
    new_ffn = jnp.moveaxis(nf, -2, -3).reshape(b, FFN_CONV_TAPS - 1, -1)
    new_conv = qkv.reshape(b, t, -1)[:, t - (GDN_CONV_TAPS - 1):, :]
    return (y.reshape(b, t, d), c_new.reshape(b, t, -1), kpe_new.reshape(b, t, -1), s_new,
            new_conv, new_ffn)


def kernel(x_prompt, x_sample, cache_ckv, cache_kpe, page_table, state_gdn, state_gdn_conv,
           state_ffn_conv, w_in, gdn_conv_w, gdn_A_log, gdn_dt_bias, gdn_norm_w, mla_q_norm_w,
           mla_w_uq, mla_kv_norm_w, mla_w_uk, mla_w_uv, w_out, ln1_g, ln1_b, ffn_w_up, ffn_conv_w,
           ffn_w_down, ln2_g, ln2_b):
    depth = w_in.shape[0]
    alpha = (2.0 * depth) ** 0.25
    bp, tp, _ = x_prompt.shape
    bs, ts, _ = x_sample.shape
    assert ts == SUBLANES, "sample group: one 8-row tile per sequence"
    xp, xs = x_prompt, x_sample
    new_p, new_s = [], []
    for l in range(depth):
        wts = _prep_weights(w_in[l], gdn_conv_w[l], gdn_A_log[l], gdn_dt_bias[l], gdn_norm_w[l],
                            mla_q_norm_w[l], mla_w_uq[l], mla_kv_norm_w[l], mla_w_uk[l],
                            mla_w_uv[l], w_out[l], ln1_g[l], ln1_b[l], ffn_w_up[l], ffn_conv_w[l],
                            ffn_w_down[l], ln2_g[l], ln2_b[l])
        two_ff = ffn_w_up.shape[-1]
        xp, *st_p = _layer(
            xp, None, l, jnp.zeros((bp, GDN_H, GDN_D, GDN_D), F32),
            jnp.zeros((bp, GDN_CONV_TAPS - 1, GDN_QKV), F32),
            jnp.zeros((bp, FFN_CONV_TAPS - 1, two_ff), F32), wts, alpha)
        xs, *st_s = _layer(
            xs, (cache_ckv, cache_kpe, page_table), l, state_gdn[l], state_gdn_conv[l],
            state_ffn_conv[l], wts, alpha)
        new_p.append(st_p)
        new_s.append(st_s)
    stack = lambda sts: [jnp.stack(v) for v in zip(*sts)]
    return (xp, xs, *stack(new_p), *stack(new_s))
```

```python
import functools
import math

import jax
import jax.numpy as jnp
import numpy as np
from jax import lax
from jax.experimental import pallas as pl
from jax.experimental.pallas import tpu as pltpu

F32 = jnp.float32
BF16 = jnp.bfloat16

GDN_H = 8
GDN_D = 64
GDN_QK = GDN_H * GDN_D
GDN_QKV = 3 * GDN_QK
GDN_CONV_TAPS = 4
GDN_CHUNK = 64
GDN_GROUP = 256
GDN_GROUPS = GDN_QK // GDN_GROUP
HEADS_PER_GROUP = GDN_GROUP // GDN_D
MLA_H = 8
MLA_Q_RANK = 256
MLA_KV_RANK = 128
MLA_NOPE_D = 64
MLA_ROPE_D = 32
MLA_V_D = 64
MLA_QK = MLA_KV_RANK + MLA_ROPE_D
MLA_SOFTMAX_SCALE = (MLA_NOPE_D + MLA_ROPE_D) ** -0.5
LOG2_E = math.log2(math.e)
CT_ROWS = MLA_KV_RANK + 16
ROPE_BASE = 10000.0
FFN_CONV_TAPS = 3
RMS_EPS = 1e-6
LN_EPS = 1e-5
L2_EPS = 1e-6

SUBLANES = 8
LANES = 128
VMEM_LIMIT_BYTES = 56 * 1024 * 1024
FFN_CHUNK = 256
SAMPLE_PAGES_PER_STEP = 64

_NT = (((1,), (1,)), ((), ()))
_TN = (((0,), (0,)), ((), ()))


def _mm(a, b):
    return jnp.dot(a.astype(BF16), b.astype(BF16), preferred_element_type=F32)


def _mm_nt(a, b):
    return lax.dot_general(a.astype(BF16), b.astype(BF16), _NT, preferred_element_type=F32)


def _sigmoid(x):
    return 1.0 / (1.0 + jnp.exp(-x))


def _silu(x):
    return x * _sigmoid(x)


def _rms(x, w):
    return x * lax.rsqrt(jnp.mean(x * x, axis=-1, keepdims=True) + RMS_EPS) * w


def _layernorm(x, g, b):
    mu = jnp.mean(x, axis=-1, keepdims=True)
    xc = x - mu
    var = jnp.mean(xc * xc, axis=-1, keepdims=True)
    return xc * lax.rsqrt(var + LN_EPS) * g + b


def _eye(n, dtype):
    return (lax.broadcasted_iota(jnp.int32, (n, n), 0)
            == lax.broadcasted_iota(jnp.int32, (n, n), 1)).astype(dtype)


def _const_spec(shape):
    nd = len(shape)
    return pl.BlockSpec(shape, lambda *_: (0,) * nd, pipeline_mode=pl.Buffered(1))


def _in_proj_kernel(x_ref, w_qkv_ref, w_z_ref, w_rest_ref, w_q_ref, w_uk_ref, qn_ref, kvn_ref,
                    *refs, transposed_q):
    if transposed_q:
        (cos_ref, sin_ref, cos_k_ref, sin_k_ref,
         qkv_ref, z_ref, small_ref, ckv_ref, kpe_ref, q_ref, k_ref, ct_ref) = refs
    else:
        cos_ref, sin_ref, qkv_ref, z_ref, small_ref, ckv_ref, kpe_ref, q_ref, k_ref = refs
    xb = x_ref[...].astype(BF16)
    qkv_ref[...] = jnp.dot(xb, w_qkv_ref[...], preferred_element_type=F32)
    z_ref[...] = jnp.dot(xb, w_z_ref[...], preferred_element_type=F32)
    rest = jnp.dot(xb, w_rest_ref[...], preferred_element_type=F32)
    cq = rest[:, 0:MLA_Q_RANK]
    ckv_raw = rest[:, MLA_Q_RANK:MLA_Q_RANK + MLA_KV_RANK]
    small = rest[:, MLA_Q_RANK + MLA_KV_RANK:]
    small_ref[...] = small

    c_new = _rms(ckv_raw, kvn_ref[...])
    ckv_ref[...] = c_new
    cqn = _rms(cq, qn_ref[...])
    n_nope = MLA_H * MLA_NOPE_D
    npe = MLA_H * MLA_ROPE_D
    k_rot_a = small[:, 16:16 + MLA_ROPE_D]
    k_rot_b = small[:, 16 + MLA_ROPE_D:16 + 2 * MLA_ROPE_D]

    if transposed_q:
        cos_t = cos_ref[...]
        sin_t = sin_ref[...]
        tm = x_ref.shape[0]
        kpe = k_rot_a * cos_k_ref[...] + k_rot_b * sin_k_ref[...]
        ct_ref[0:MLA_KV_RANK, :] = lax.dot_general(_eye(LANES, BF16), c_new.astype(BF16), _NT,
                                                   preferred_element_type=F32).astype(ct_ref.dtype)
        ct_ref[MLA_KV_RANK:, :] = jnp.ones((CT_ROWS - MLA_KV_RANK, tm), ct_ref.dtype)
        q_t = lax.dot_general(w_q_ref[...], cqn.astype(BF16), _NT,
                              preferred_element_type=F32)
        scale = MLA_SOFTMAX_SCALE * LOG2_E
        q_pe = (q_t[n_nope:n_nope + npe] * cos_t + q_t[n_nope + npe:] * sin_t) * scale
        for h in range(MLA_H):
            q_lat = _mm(w_uk_ref[h], q_t[h * MLA_NOPE_D:(h + 1) * MLA_NOPE_D]) * scale
            cols = slice(h * tm, (h + 1) * tm)
            q_ref[0:MLA_KV_RANK, cols] = q_lat.astype(q_ref.dtype)
            q_ref[MLA_KV_RANK:MLA_QK, cols] = q_pe[h * MLA_ROPE_D:(h + 1) * MLA_ROPE_D].astype(q_ref.dtype)
    else:
        cos = cos_ref[...]
        sin = sin_ref[...]
        kpe = k_rot_a * cos[:, 0:MLA_ROPE_D] + k_rot_b * sin[:, 0:MLA_ROPE_D]
        q = _mm(cqn, w_q_ref[...])
        q_pe = (q[:, n_nope:n_nope + npe] * cos + q[:, n_nope + npe:] * sin) * MLA_SOFTMAX_SCALE
        for h in range(MLA_H):
            q_lat = _mm(q[:, h * MLA_NOPE_D:(h + 1) * MLA_NOPE_D], w_uk_ref[h]) * MLA_SOFTMAX_SCALE
            q_ref[h, :, 0:MLA_KV_RANK] = q_lat.astype(q_ref.dtype)
            q_ref[h, :, MLA_KV_RANK:MLA_QK] = q_pe[:, h * MLA_ROPE_D:(h + 1) * MLA_ROPE_D].astype(q_ref.dtype)

    kpe_ref[...] = kpe
    k_ref[:, 0:MLA_KV_RANK] = c_new.astype(k_ref.dtype)
    k_ref[:, MLA_KV_RANK:MLA_QK] = kpe.astype(k_ref.dtype)


def _in_proj(x2, wts, rope, tm, transposed_q):
    m, d = x2.shape
    row = lambda i: (i, 0)
    npe = MLA_H * MLA_ROPE_D
    act_dtype = BF16 if transposed_q else F32
    out_shape = [
        jax.ShapeDtypeStruct((m, GDN_QKV), F32),
        jax.ShapeDtypeStruct((m, GDN_QK), F32),
        jax.ShapeDtypeStruct((m, LANES), F32),
        jax.ShapeDtypeStruct((m, MLA_KV_RANK), F32),
        jax.ShapeDtypeStruct((m, MLA_ROPE_D), F32),
    ]
    out_specs = [
        pl.BlockSpec((tm, GDN_QKV), row), pl.BlockSpec((tm, GDN_QK), row),
        pl.BlockSpec((tm, LANES), row), pl.BlockSpec((tm, MLA_KV_RANK), row),
        pl.BlockSpec((tm, MLA_ROPE_D), row),
    ]
    if transposed_q:
        n_rope_tiles = rope[0].shape[1] // tm
        rope_specs = [pl.BlockSpec((npe, tm), lambda i: (0, i % n_rope_tiles))] * 2
        rope_specs += [pl.BlockSpec((tm, MLA_ROPE_D), lambda i: (i % n_rope_tiles, 0))] * 2
        w_q, w_uk = wts["w_q_t"], wts["w_uk"]
        out_shape += [jax.ShapeDtypeStruct((m // tm, MLA_QK, MLA_H * tm), act_dtype),
                      jax.ShapeDtypeStruct((m, MLA_QK), act_dtype),
                      jax.ShapeDtypeStruct((CT_ROWS, m), act_dtype)]
        out_specs += [pl.BlockSpec((None, MLA_QK, MLA_H * tm), lambda i: (i, 0, 0)),
                      pl.BlockSpec((tm, MLA_QK), row),
                      pl.BlockSpec((CT_ROWS, tm), lambda i: (0, i))]
    else:
        n_rope_tiles = rope[0].shape[0] // tm
        rope_specs = [pl.BlockSpec((tm, npe), lambda i: (i % n_rope_tiles, 0))] * 2
        w_q, w_uk = wts["w_q"], wts["w_uk_t"]
        out_shape += [jax.ShapeDtypeStruct((MLA_H, m, MLA_QK), act_dtype),
                      jax.ShapeDtypeStruct((m, MLA_QK), act_dtype)]
        out_specs += [pl.BlockSpec((MLA_H, tm, MLA_QK), lambda i: (0, i, 0)),
                      pl.BlockSpec((tm, MLA_QK), row)]
    return pl.pallas_call(
        functools.partial(_in_proj_kernel, transposed_q=transposed_q),
        grid=(m // tm,),
        in_specs=[
            pl.BlockSpec((tm, d), row),
            _const_spec(wts["w_qkv"].shape), _const_spec(wts["w_z"].shape),
            _const_spec(wts["w_rest"].shape), _const_spec(w_q.shape), _const_spec(w_uk.shape),
            _const_spec(wts["q_norm"].shape), _const_spec(wts["kv_norm"].shape),
            *rope_specs,
        ],
        out_specs=tuple(out_specs),
        out_shape=tuple(out_shape),
        compiler_params=pltpu.CompilerParams(
            dimension_semantics=("parallel",), vmem_limit_bytes=VMEM_LIMIT_BYTES),
        name="in_proj",
    )(x2, wts["w_qkv"], wts["w_z"], wts["w_rest"], w_q, w_uk, wts["q_norm"], wts["kv_norm"],
      *rope)


def _gdn_kernel(qkv_ref, z_ref, small_ref, hist_ref, s0_ref, convw_ref, gpar_ref, normw_ref,
                tril_ref, bones_ref, eb_ref, eg_ref,
                o_ref, sout_ref, xbuf, s_scr, o_buf, *, t_in, tt):
    t = pl.program_id(1)
    c_rows, gw = GDN_CHUNK, GDN_GROUP
    padded = tt > t_in

    @pl.when(t == 0)
    def _():
        xbuf[0:SUBLANES, :] = hist_ref[0]
        s_scr[...] = jnp.zeros(s_scr.shape, F32)
        for h in range(GDN_H):
            g, hh = divmod(h, HEADS_PER_GROUP)
            s_scr[g, hh * GDN_D:(hh + 1) * GDN_D, hh * GDN_D:(hh + 1) * GDN_D] = s0_ref[0, h]

    if padded:
        xbuf[SUBLANES + t_in:SUBLANES + tt, :] = jnp.zeros((tt - t_in, GDN_QKV), F32)
    xbuf[SUBLANES:SUBLANES + t_in, :] = qkv_ref[0]
    cw = convw_ref[...]
    y = xbuf[pl.ds(SUBLANES - 3, tt), :] * cw[0:1]
    for j in range(1, GDN_CONV_TAPS):
        y = y + xbuf[pl.ds(SUBLANES - 3 + j, tt), :] * cw[j:j + 1]
    qkv = _silu(y)
    xbuf[0:SUBLANES, :] = xbuf[t_in:t_in + SUBLANES, :]

    small = small_ref[0]
    z = z_ref[0]
    if padded:
        row_valid = lax.broadcasted_iota(jnp.int32, (tt, 1), 0) < t_in
        qkv = jnp.where(row_valid, qkv, 0.0)
        small = jnp.concatenate([small, jnp.zeros((tt - t_in, LANES), F32)], axis=0)
        z = jnp.concatenate([z, jnp.zeros((tt - t_in, GDN_QK), F32)], axis=0)
    beta_all = _sigmoid(small)
    xa = small + gpar_ref[1:2, :]
    softplus = jnp.maximum(xa, 0.0) + jnp.log1p(jnp.exp(-jnp.abs(xa)))
    g_all = -jnp.exp(gpar_ref[0:1, :]) * softplus
    if padded:
        beta_all = jnp.where(row_valid, beta_all, 0.0)
        g_all = jnp.where(row_valid, g_all, 0.0)

    def split3(x):
        hi = x.astype(BF16)
        r1 = x - hi.astype(F32)
        mid = r1.astype(BF16)
        return hi, mid, (r1 - mid.astype(F32)).astype(BF16)

    def mm_exact(x, ones_mat, terms=3):
        return sum(jnp.dot(part, ones_mat, preferred_element_type=F32) for part in split3(x)[:terms])

    def mm_exact_left(ones_mat, x):
        return sum(jnp.dot(ones_mat, part, preferred_element_type=F32) for part in split3(x))

    gc_all = mm_exact_left(tril_ref[...], g_all)
    beta_x = mm_exact(beta_all, eb_ref[...])
    gc_x = mm_exact(gc_all, eg_ref[...])
    block_ones = bones_ref[...]

    def block_diag(x):
        return jnp.concatenate([x.astype(BF16)] * HEADS_PER_GROUP, axis=0) * block_ones

    rr = lax.broadcasted_iota(jnp.int32, (c_rows, gw), 0)
    cc = lax.broadcasted_iota(jnp.int32, (c_rows, gw), 1) % c_rows
    incl, strict, diag = rr >= cc, rr > cc, rr == cc
    eye_t = diag.astype(F32)
    n_levels = int(math.log2(c_rows)) - 1
    n_chunks = tt // c_rows

    q_n, k_n = [], []
    for g in range(GDN_GROUPS):
        q_g = qkv[:, g * gw:(g + 1) * gw]
        k_g = qkv[:, GDN_QK + g * gw:GDN_QK + (g + 1) * gw]
        q_n.append(q_g * lax.rsqrt(mm_exact(q_g * q_g, block_ones, 2) + L2_EPS) * (GDN_D ** -0.5))
        k_n.append(k_g * lax.rsqrt(mm_exact(k_g * k_g, block_ones, 2) + L2_EPS))

    units = [(g, c) for c in range(n_chunks) for g in range(GDN_GROUPS)]
    st = {}
    for g, c in units:
        rows = slice(c * c_rows, (c + 1) * c_rows)
        lanes = slice(g * gw, (g + 1) * gw)
        q_c, k_c = q_n[g][rows], k_n[g][rows]
        v_c = qkv[rows, 2 * GDN_QK + g * gw:2 * GDN_QK + (g + 1) * gw]
        b_c = beta_x[rows, lanes]
        gc_c = gc_x[rows, lanes]
        g_row = jnp.sum(jnp.where(diag, gc_c, 0.0), axis=0, keepdims=True)
        g_last = gc_c[c_rows - 1:c_rows, :]
        decay = jnp.exp(jnp.where(incl, gc_c - g_row, -jnp.inf))
        e_col = jnp.exp(gc_c)
        kb = k_c * b_c
        kk_qk = lax.dot_general(jnp.concatenate([kb, q_c], axis=0).astype(BF16), block_diag(k_c),
                                _NT, preferred_element_type=F32)
        lmat = jnp.where(strict, kk_qk[0:c_rows] * decay, 0.0)
        st[g, c] = dict(
            inv=eye_t - lmat, lmat=lmat,
            a=(kk_qk[c_rows:] * decay).astype(BF16),
            vb=block_diag(v_c * b_c), kbe=block_diag(kb * e_col),
            qg=q_c * e_col,
            kg=(k_c * jnp.exp(g_last - gc_c)).astype(BF16),
            s_decay=jnp.exp(g_last),
        )
    for u in units:
        st[u]["pw"] = _mm(st[u]["lmat"], block_diag(st[u]["lmat"]))
    for lvl in range(n_levels):
        for u in units:
            s_u = st[u]
            bd = block_diag(s_u["pw"])
            if lvl < n_levels - 1:
                both = _mm(jnp.concatenate([s_u["pw"], s_u["inv"]], axis=0), bd)
                s_u["pw"] = both[0:c_rows]
                s_u["inv"] = s_u["inv"] + both[c_rows:]
            else:
                s_u["inv"] = s_u["inv"] + _mm(s_u["inv"], bd)
    for u in units:
        inv_b = st[u]["inv"].astype(BF16)
        st[u]["u"] = jnp.dot(inv_b, st[u]["vb"], preferred_element_type=F32)
        st[u]["w"] = jnp.dot(inv_b, st[u]["kbe"], preferred_element_type=F32)

    s_bd = [s_scr[g] for g in range(GDN_GROUPS)]
    block_ones_f = block_ones.astype(F32)
    for g, c in units:
        p = st[g, c]
        both = _mm(jnp.concatenate([p["w"], p["qg"]], axis=0), s_bd[g])
        v_new = p["u"] - both[0:c_rows]
        o_buf[c * c_rows:(c + 1) * c_rows, g * gw:(g + 1) * gw] = (
            both[c_rows:] + jnp.dot(p["a"], block_diag(v_new), preferred_element_type=F32))
        upd = lax.dot_general(p["kg"], v_new.astype(BF16), _TN, preferred_element_type=F32)
        s_bd[g] = s_bd[g] * p["s_decay"] + upd * block_ones_f
    for g in range(GDN_GROUPS):
        s_scr[g] = s_bd[g]

    norm_w = jnp.concatenate([normw_ref[...]] * HEADS_PER_GROUP, axis=1)
    outs = []
    for g in range(GDN_GROUPS):
        o_g = o_buf[:, g * gw:(g + 1) * gw]
        ms = mm_exact(o_g * o_g, block_ones, 2) * (1.0 / GDN_D)
        outs.append(o_g * lax.rsqrt(ms + RMS_EPS) * norm_w * _silu(z[:, g * gw:(g + 1) * gw]))
    o_ref[0] = jnp.concatenate(outs, axis=1)[0:t_in].astype(o_ref.dtype)

    @pl.when(t == pl.num_programs(1) - 1)
    def _():
        for h in range(GDN_H):
            g, hh = divmod(h, HEADS_PER_GROUP)
            sout_ref[0, h] = s_scr[g, hh * GDN_D:(hh + 1) * GDN_D, hh * GDN_D:(hh + 1) * GDN_D]


def _gdn_constants(tt):
    tril = np.kron(np.eye(tt // GDN_CHUNK), np.tril(np.ones((GDN_CHUNK, GDN_CHUNK))))
    block_ones = np.kron(np.eye(HEADS_PER_GROUP), np.ones((GDN_D, GDN_D)))
    spread = np.kron(np.eye(GDN_H), np.ones((1, GDN_D)))
    e_b = np.zeros((LANES, GDN_QK))
    e_g = np.zeros((LANES, GDN_QK))
    e_b[0:GDN_H] = spread
    e_g[GDN_H:2 * GDN_H] = spread
    return [jnp.asarray(a, BF16) for a in (tril, block_ones, e_b, e_g)]


def _gdn(qkv, z, small, hist8, s0, conv_w, gpar, norm_w, t_in, act_dtype):
    b, t, _ = qkv.shape
    tt = max(t_in, GDN_CHUNK)
    consts = _gdn_constants(tt)
    seq = lambda i, j: (i, j, 0)
    per_b3 = lambda i, j: (i, 0, 0)
    per_b4 = lambda i, j: (i, 0, 0, 0)
    return pl.pallas_call(
        functools.partial(_gdn_kernel, t_in=t_in, tt=tt),
        grid=(b, t // t_in),
        in_specs=[
            pl.BlockSpec((1, t_in, GDN_QKV), seq), pl.BlockSpec((1, t_in, GDN_QK), seq),
            pl.BlockSpec((1, t_in, LANES), seq),
            pl.BlockSpec((1, SUBLANES, GDN_QKV), per_b3),
            pl.BlockSpec((1, GDN_H, GDN_D, GDN_D), per_b4),
            _const_spec(conv_w.shape), _const_spec(gpar.shape), _const_spec(norm_w.shape),
            *[_const_spec(c.shape) for c in consts],
        ],
        out_specs=(pl.BlockSpec((1, t_in, GDN_QK), seq),
                   pl.BlockSpec((1, GDN_H, GDN_D, GDN_D), per_b4)),
        out_shape=(jax.ShapeDtypeStruct((b, t, GDN_QK), act_dtype),
                   jax.ShapeDtypeStruct((b, GDN_H, GDN_D, GDN_D), F32)),
        scratch_shapes=[pltpu.VMEM((tt + SUBLANES, GDN_QKV), F32),
                        pltpu.VMEM((GDN_GROUPS, GDN_GROUP, GDN_GROUP), F32),
                        pltpu.VMEM((tt, GDN_QK), F32)],
        compiler_params=pltpu.CompilerParams(
            dimension_semantics=("parallel", "arbitrary"), vmem_limit_bytes=VMEM_LIMIT_BYTES),
        name="gdn",
    )(qkv, z, small, hist8, s0, conv_w, gpar, norm_w, *consts)


def _attn_prompt_kernel(qi_ref, kj_ref, qt_ref, k_ref, ct_ref, w_uvt_ref, o_ref, m_scr, acc_scr,
                        *, tq, tk):
    p = pl.program_id(1)
    i = qi_ref[p]
    j = kj_ref[p]
    kv_per_q = tq // tk
    first_diag = i * kv_per_q

    @pl.when(j == 0)
    def _():
        m_scr[...] = jnp.full(m_scr.shape, -jnp.inf, F32)
        acc_scr[...] = jnp.zeros(acc_scr.shape, F32)

    def update(masked):
        s = jnp.dot(k_ref[...], qt_ref[...], preferred_element_type=F32)
        if masked:
            key = lax.broadcasted_iota(jnp.int32, s.shape, 0) + (j - first_diag) * tk
            qry = lax.broadcasted_iota(jnp.int32, s.shape, 1) % tq
            s = jnp.where(key <= qry, s, -jnp.inf)
        m_prev = m_scr[...]
        m_new = jnp.maximum(m_prev, jnp.max(s, axis=0, keepdims=True))
        alpha = jnp.exp2(m_prev - m_new)
        pmat = jnp.exp2(s - m_new)
        acc_scr[...] = alpha * acc_scr[...] + jnp.dot(ct_ref[...], pmat.astype(BF16),
                                                      preferred_element_type=F32)
        m_scr[...] = m_new

    @pl.when(j < first_diag)
    def _():
        update(False)

    @pl.when(j >= first_diag)
    def _():
        update(True)

    @pl.when(j == first_diag + kv_per_q - 1)
    def _():
        o_lat_t = acc_scr[0:MLA_KV_RANK, :] / acc_scr[MLA_KV_RANK:MLA_KV_RANK + 1, :]
        outs = [_mm(w_uvt_ref[h], o_lat_t[:, h * tq:(h + 1) * tq]) for h in range(MLA_H)]
        o_ref[...] = jnp.concatenate(outs, axis=0).T.astype(o_ref.dtype)


def _attn_prompt(q_t, k, c_t, w_uv_t, b, t, tq, tk):
    nq, nk = t // tq, t // tk
    kv_per_q = tq // tk
    m = b * t
    assert q_t.shape == (m // tq, MLA_QK, MLA_H * tq), "query tiles must match the projection tiles"
    pairs = [(i, j) for i in range(nq) for j in range((i + 1) * kv_per_q)]
    qi = jnp.asarray(np.array([p[0] for p in pairs], np.int32))
    kj = jnp.asarray(np.array([p[1] for p in pairs], np.int32))
    grid_spec = pltpu.PrefetchScalarGridSpec(
        num_scalar_prefetch=2,
        grid=(b, len(pairs)),
        in_specs=[
            pl.BlockSpec((None, MLA_QK, MLA_H * tq), lambda bi, p, qi, kj: (bi * nq + qi[p], 0, 0)),
            pl.BlockSpec((tk, MLA_QK), lambda bi, p, qi, kj: (bi * nk + kj[p], 0)),
            pl.BlockSpec((CT_ROWS, tk), lambda bi, p, qi, kj: (0, bi * nk + kj[p])),
            pl.BlockSpec(w_uv_t.shape, lambda bi, p, qi, kj: (0, 0, 0)),
        ],
        out_specs=pl.BlockSpec((tq, MLA_H * MLA_V_D), lambda bi, p, qi, kj: (bi * nq + qi[p], 0)),
        scratch_shapes=[pltpu.VMEM((1, MLA_H * tq), F32),
                        pltpu.VMEM((CT_ROWS, MLA_H * tq), F32)],
    )
    return pl.pallas_call(
        functools.partial(_attn_prompt_kernel, tq=tq, tk=tk),
        grid_spec=grid_spec,
        out_shape=jax.ShapeDtypeStruct((m, MLA_H * MLA_V_D), BF16),
        compiler_params=pltpu.CompilerParams(
            dimension_semantics=("parallel", "arbitrary"), vmem_limit_bytes=VMEM_LIMIT_BYTES),
        name="attn_prompt",
    )(qi, kj, q_t, k, c_t, w_uv_t)


def _softmax_update(s, v_b, m_scr, l_scr, acc_scr):
    m_prev = m_scr[...]
    m_new = jnp.maximum(m_prev, jnp.max(s, axis=-1, keepdims=True))
    alpha = jnp.exp(m_prev - m_new)
    p = jnp.exp(s - m_new)
    l_scr[...] = alpha * l_scr[...] + jnp.sum(p, axis=-1, keepdims=True)
    acc_scr[...] = alpha * acc_scr[...] + jnp.dot(p.astype(BF16), v_b, preferred_element_type=F32)
    m_scr[...] = m_new


def _attn_sample_kernel(pt_ref, q_ref, knew_ref, w_uv_ref, ckv_hbm, kpe_hbm, o_ref,
                        c_buf, kpe_buf, sems, m_scr, l_scr, acc_scr, *,
                        layer, pages_per_step, page, t_new):
    seq = pl.program_id(0)
    grp = pl.program_id(1)
    n_grp = pl.num_programs(1)
    step = seq * n_grp + grp
    slot = step % 2
    rows = MLA_H * t_new

    def page_copies(seq_i, grp_i, slot_i, p):
        pid = pt_ref[seq_i, grp_i * pages_per_step + p]
        return (
            pltpu.make_async_copy(ckv_hbm.at[layer, pid],
                                  c_buf.at[slot_i, pl.ds(p * page, page), :], sems.at[0, slot_i]),
            pltpu.make_async_copy(kpe_hbm.at[layer, pid],
                                  kpe_buf.at[slot_i, :, pl.ds(p * page, page)], sems.at[1, slot_i]),
        )

    def start_fetch(seq_i, grp_i, slot_i):
        for p in range(pages_per_step):
            for cp in page_copies(seq_i, grp_i, slot_i, p):
                cp.start()

    @pl.when(step == 0)
    def _():
        start_fetch(seq, grp, slot)

    @pl.when(step + 1 < pl.num_programs(0) * n_grp)
    def _():
        nxt = step + 1
        start_fetch(nxt // n_grp, nxt % n_grp, 1 - slot)

    @pl.when(grp == 0)
    def _():
        m_scr[...] = jnp.full(m_scr.shape, -jnp.inf, F32)
        l_scr[...] = jnp.zeros(l_scr.shape, F32)
        acc_scr[...] = jnp.zeros(acc_scr.shape, F32)

    for p in range(pages_per_step):
        for cp in page_copies(seq, grp, slot, p):
            cp.wait()

    q = q_ref[...].reshape(rows, MLA_QK).astype(BF16)
    c_b = c_buf[slot].astype(BF16)
    s = (lax.dot_general(q[:, 0:MLA_KV_RANK], c_b, _NT, preferred_element_type=F32)
         + jnp.dot(q[:, MLA_KV_RANK:MLA_QK], kpe_buf[slot].astype(BF16), preferred_element_type=F32))
    _softmax_update(s, c_b, m_scr, l_scr, acc_scr)

    @pl.when(grp == n_grp - 1)
    def _():
        k_new = knew_ref[...]
        s = _mm_nt(q, k_new)
        qi = lax.broadcasted_iota(jnp.int32, (rows, t_new), 0) % t_new
        ki = lax.broadcasted_iota(jnp.int32, (rows, t_new), 1)
        s = jnp.where(ki <= qi, s, -jnp.inf)
        _softmax_update(s, k_new[:, 0:MLA_KV_RANK].astype(BF16), m_scr, l_scr, acc_scr)
        o_lat = acc_scr[...] / l_scr[...]
        outs = [_mm(o_lat[h * t_new:(h + 1) * t_new], w_uv_ref[h]) for h in range(MLA_H)]
        o_ref[...] = jnp.concatenate(outs, axis=-1).astype(o_ref.dtype)


def _attn_sample(q, k_new, cache_ckv, cache_kpe, layer, page_table, w_uv, b, t_new):
    cache_kpe = jnp.swapaxes(cache_kpe, 2, 3)
    n_pages = page_table.shape[1]
    page = cache_ckv.shape[2]
    pages_per_step = math.gcd(SAMPLE_PAGES_PER_STEP, n_pages)
    rows = MLA_H * t_new
    n_steps = n_pages // pages_per_step

    keys = pages_per_step * page
    grid_spec = pltpu.PrefetchScalarGridSpec(
        num_scalar_prefetch=1,
        grid=(b, n_steps),
        in_specs=[
            pl.BlockSpec((MLA_H, t_new, MLA_QK), lambda bi, s, pt: (0, bi, 0)),
            pl.BlockSpec((t_new, MLA_QK), lambda bi, s, pt: (bi, 0)),
            pl.BlockSpec(w_uv.shape, lambda bi, s, pt: (0, 0, 0)),
            pl.BlockSpec(memory_space=pl.ANY),
            pl.BlockSpec(memory_space=pl.ANY),
        ],
        out_specs=pl.BlockSpec((t_new, MLA_H * MLA_V_D), lambda bi, s, pt: (bi, 0)),
        scratch_shapes=[pltpu.VMEM((2, keys, MLA_KV_RANK), F32),
                        pltpu.VMEM((2, MLA_ROPE_D, keys), F32),
                        pltpu.SemaphoreType.DMA((2, 2)),
                        pltpu.VMEM((rows, 1), F32), pltpu.VMEM((rows, 1), F32),
                        pltpu.VMEM((rows, MLA_KV_RANK), F32)],
    )
    return pl.pallas_call(
        functools.partial(_attn_sample_kernel, layer=layer, pages_per_step=pages_per_step,
                          page=page, t_new=t_new),
        grid_spec=grid_spec,
        out_shape=jax.ShapeDtypeStruct((b * t_new, MLA_H * MLA_V_D), F32),
        compiler_params=pltpu.CompilerParams(
            dimension_semantics=("arbitrary", "arbitrary"), vmem_limit_bytes=VMEM_LIMIT_BYTES),
        name="attn_sample",
    )(page_table, q, k_new, w_uv, cache_ckv, cache_kpe)


def _ffn_kernel(x_ref, og_ref, om_ref, hist_ref, w_out_ref, ln_ref, w_up_ref, cw_ref, w_dn_ref,
                y_ref, nf_ref, *scratch, tm, alpha, short_seq):
    d_gdn = og_ref.shape[-1]
    n_chunks, _, two_fc = w_up_ref.shape
    fc = two_fc // 2
    mix = _mm(og_ref[...], w_out_ref[0:d_gdn, :]) + _mm(om_ref[...], w_out_ref[d_gdn:, :])
    x1 = _layernorm(alpha * x_ref[...] + mix, ln_ref[0:1, :], ln_ref[1:2, :])
    x1_b = x1.astype(BF16)

    if short_seq:
        h_buf, = scratch
        nb = tm // SUBLANES
        t_pos = lax.broadcasted_iota(jnp.int32, (nb, SUBLANES, two_fc), 1)
    else:
        h_buf, ubuf, carry = scratch

        @pl.when(pl.program_id(1) == 0)
        def _():
            carry[...] = hist_ref[0]

    for c in range(n_chunks):
        cols = slice(c * two_fc, (c + 1) * two_fc)
        up = jnp.dot(x1_b, w_up_ref[c], preferred_element_type=F32)
        cw = cw_ref[c]
        if short_seq:
            up3 = up.reshape(nb, SUBLANES, two_fc)
            h0 = hist_ref[:, SUBLANES - 2:SUBLANES - 1, cols]
            h1 = hist_ref[:, SUBLANES - 1:SUBLANES, cols]
            prev1 = jnp.where(t_pos == 0, h1, pltpu.roll(up3, 1, axis=1))
            prev2 = jnp.where(t_pos == 0, h0, jnp.where(t_pos == 1, h1, pltpu.roll(up3, 2, axis=1)))
            hc = (prev2 * cw[0:1] + prev1 * cw[1:2] + up3 * cw[2:3]).reshape(tm, two_fc)
            nf_ref[:, :, cols] = up3
        else:
            ubuf[0:SUBLANES, :] = carry[:, cols]
            ubuf[SUBLANES:SUBLANES + tm, :] = up
            hc = (ubuf[pl.ds(SUBLANES - 2, tm), :] * cw[0:1]
                  + ubuf[pl.ds(SUBLANES - 1, tm), :] * cw[1:2] + up * cw[2:3])
            last = ubuf[tm:tm + SUBLANES, :]
            carry[:, cols] = last
            nf_ref[0, :, cols] = last
        h_buf[:, c * fc:(c + 1) * fc] = (_silu(hc[:, fc:]) * hc[:, 0:fc]).astype(BF16)
    down = jnp.dot(h_buf[...], w_dn_ref[...], preferred_element_type=F32)
    y_ref[...] = _layernorm(alpha * x1 + down, ln_ref[2:3, :], ln_ref[3:4, :])


def _ffn(x2, o_gdn, o_mla, hist8, wts, b, t, tm, alpha):
    m, d = x2.shape
    n_chunks, _, two_fc = wts["w_up"].shape
    two_ff = n_chunks * two_fc
    short_seq = t == SUBLANES
    if short_seq:
        grid = (m // tm, 1)
        nb = tm // SUBLANES
        row = lambda i, j: (i, 0)
        hist_spec = pl.BlockSpec((nb, SUBLANES, two_ff), lambda i, j: (i, 0, 0))
        scratch = [pltpu.VMEM((tm, two_ff // 2), BF16)]
    else:
        nt = t // tm
        grid = (b, nt)
        row = lambda i, j: (i * nt + j, 0)
        hist_spec = pl.BlockSpec((1, SUBLANES, two_ff), lambda i, j: (i, 0, 0))
        scratch = [pltpu.VMEM((tm, two_ff // 2), BF16),
                   pltpu.VMEM((tm + SUBLANES, two_fc), F32), pltpu.VMEM((SUBLANES, two_ff), F32)]
    return pl.pallas_call(
        functools.partial(_ffn_kernel, tm=tm, alpha=alpha, short_seq=short_seq),
        grid=grid,
        in_specs=[
            pl.BlockSpec((tm, d), row), pl.BlockSpec((tm, o_gdn.shape[1]), row),
            pl.BlockSpec((tm, o_mla.shape[1]), row), hist_spec,
            _const_spec(wts["w_out"].shape), _const_spec(wts["ln"].shape),
            _const_spec(wts["w_up"].shape), _const_spec(wts["ffn_cw"].shape),
            _const_spec(wts["w_dn"].shape),
        ],
        out_specs=(pl.BlockSpec((tm, d), row), hist_spec),
        out_shape=(jax.ShapeDtypeStruct((m, d), F32),
                   jax.ShapeDtypeStruct((b, SUBLANES, two_ff), F32)),
        scratch_shapes=scratch,
        compiler_params=pltpu.CompilerParams(
            dimension_semantics=("parallel", "arbitrary"), vmem_limit_bytes=VMEM_LIMIT_BYTES),
        name="ffn",
    )(x2, o_gdn, o_mla, hist8, wts["w_out"], wts["ln"], wts["w_up"], wts["ffn_cw"], wts["w_dn"])


def _swap_halves(w):
    half = w.shape[-1] // 2
    return jnp.concatenate([-w[..., half:], w[..., :half]], axis=-1)


def _prep_weights(w_in, gdn_conv_w, gdn_A_log, gdn_dt_bias, gdn_norm_w, mla_q_norm_w, mla_w_uq,
                  mla_kv_norm_w, mla_w_uk, mla_w_uv, w_out, ln1_g, ln1_b, ffn_w_up, ffn_conv_w,
                  ffn_w_down, ln2_g, ln2_b):
    d_model = w_in.shape[0]
    o = 0
    w_qkv = w_in[:, o:o + GDN_QKV]; o += GDN_QKV
    w_z = w_in[:, o:o + GDN_QK]; o += GDN_QK
    w_b = w_in[:, o:o + GDN_H]; o += GDN_H
    w_a = w_in[:, o:o + GDN_H]; o += GDN_H
    w_cq = w_in[:, o:o + MLA_Q_RANK]; o += MLA_Q_RANK
    w_ckv = w_in[:, o:o + MLA_KV_RANK]; o += MLA_KV_RANK
    w_kpe = w_in[:, o:o + MLA_ROPE_D]
    pad = jnp.zeros((d_model, LANES - 2 * GDN_H - 2 * MLA_ROPE_D), w_in.dtype)
    w_small = jnp.concatenate([w_b, w_a, w_kpe, _swap_halves(w_kpe), pad], axis=1)
    w_rest = jnp.concatenate([w_cq, w_ckv, w_small], axis=1)

    uq = mla_w_uq.reshape(MLA_Q_RANK, MLA_H, MLA_NOPE_D + MLA_ROPE_D)
    uq_nope = uq[:, :, :MLA_NOPE_D].reshape(MLA_Q_RANK, MLA_H * MLA_NOPE_D)
    uq_pe = uq[:, :, MLA_NOPE_D:]
    w_q = jnp.concatenate([uq_nope, uq_pe.reshape(MLA_Q_RANK, -1),
                           _swap_halves(uq_pe).reshape(MLA_Q_RANK, -1)], axis=1)

    gpar = jnp.zeros((2, LANES), F32)
    gpar = gpar.at[0, GDN_H:2 * GDN_H].set(gdn_A_log).at[1, GDN_H:2 * GDN_H].set(gdn_dt_bias)

    d_ff = ffn_w_down.shape[0]
    n_chunks = d_ff // FFN_CHUNK

    def interleave(w):
        lead = w.shape[:-1]
        w = w.reshape(lead + (2, n_chunks, FFN_CHUNK))
        return jnp.moveaxis(w, -3, -2).reshape(lead + (n_chunks, 2 * FFN_CHUNK))

    return {
        "w_qkv": w_qkv.astype(BF16), "w_z": w_z.astype(BF16), "w_rest": w_rest.astype(BF16),
        "w_q": w_q.astype(BF16), "w_q_t": w_q.T.astype(BF16),
        "w_uk_t": jnp.transpose(mla_w_uk, (1, 2, 0)).astype(BF16),
        "w_uk": jnp.transpose(mla_w_uk, (1, 0, 2)).astype(BF16),
        "w_uv": jnp.transpose(mla_w_uv, (1, 0, 2)).astype(BF16),
        "w_uv_t": jnp.transpose(mla_w_uv, (1, 2, 0)).astype(BF16),
        "q_norm": mla_q_norm_w.reshape(1, -1), "kv_norm": mla_kv_norm_w.reshape(1, -1),
        "gdn_conv_w": gdn_conv_w, "gpar": gpar, "gdn_norm_w": gdn_norm_w.reshape(1, -1),
        "w_out": w_out.astype(BF16),
        "ln": jnp.stack([ln1_g, ln1_b, ln2_g, ln2_b]),
        "w_up": jnp.transpose(interleave(ffn_w_up), (1, 0, 2)).astype(BF16),
        "ffn_cw": jnp.transpose(interleave(ffn_conv_w), (1, 0, 2)),
        "w_dn": ffn_w_down.astype(BF16),
        "interleave": interleave, "n_chunks": n_chunks,
    }


def _rope_tables(past_len, t, tm, transposed):
    half = MLA_ROPE_D // 2
    pos = (past_len + jnp.arange(t, dtype=jnp.int32)).astype(F32)
    inv = ROPE_BASE ** (-jnp.arange(half, dtype=F32) / half)
    ang = pos[:, None] * inv
    reps = (max(tm // t, 1), 2 * MLA_H)
    cos, sin = jnp.tile(jnp.cos(ang), reps), jnp.tile(jnp.sin(ang), reps)
    if transposed:
        return cos.T, sin.T, cos[:, 0:MLA_ROPE_D], sin[:, 0:MLA_ROPE_D]
    return cos, sin


def _pad_hist(h):
    return jnp.pad(h, ((0, 0), (SUBLANES - h.shape[1], 0), (0, 0)))


def _tiles(b, t):
    if t == SUBLANES:
        rows = min(256, b * t)
        return dict(proj=rows, ffn=rows, gdn=t, attn_q=None, attn_k=None)
    return dict(proj=min(512, t), ffn=min(512, t), gdn=min(512, t), attn_q=min(512, t),
                attn_k=min(256, t))


def _layer(x, past, layer, s0, conv_hist, ffn_hist, wts, alpha):
    b, t, d = x.shape
    m = b * t
    x2 = x.reshape(m, d)
    tiles = _tiles(b, t)
    short_seq = past is not None
    past_len = past[2].shape[1] * past[0].shape[2] if short_seq else 0
    act_dtype = F32 if short_seq else BF16
    rope = _rope_tables(past_len, t, tiles["proj"], transposed=not short_seq)
    qkv, z, small, c_new, kpe_new, q, k, *c_t = _in_proj(x2, wts, rope, tiles["proj"],
                                                         transposed_q=not short_seq)

    o_gdn, s_new = _gdn(qkv.reshape(b, t, -1), z.reshape(b, t, -1), small.reshape(b, t, -1),
                        _pad_hist(conv_hist), s0, wts["gdn_conv_w"], wts["gpar"], wts["gdn_norm_w"],
                        tiles["gdn"], act_dtype)
    if short_seq:
        cache_ckv, cache_kpe, page_table = past
        o_mla = _attn_sample(q, k, cache_ckv, cache_kpe, layer, page_table, wts["w_uv"], b, t)
    else:
        o_mla = _attn_prompt(q, k, c_t[0], wts["w_uv_t"], b, t, tiles["attn_q"], tiles["attn_k"])

    hist8 = wts["interleave"](_pad_hist(ffn_hist)).reshape(b, SUBLANES, -1)
    y, nf8 = _ffn(x2, o_gdn.reshape(m, -1), o_mla, hist8, wts, b, t, tiles["ffn"], alpha)

    n_chunks = wts["n_chunks"]
    nf = nf8[:, SUBLANES - (FFN_CONV_TAPS - 1):, :].reshape(b, FFN_CONV_TAPS - 1, n_chunks, 2, FFN_CHUNK)
    new_ffn = jnp.moveaxis(nf, -2, -3).reshape(b, FFN_CONV_TAPS - 1, -1)
    new_conv = qkv.reshape(b, t, -1)[:, t - (GDN_CONV_TAPS - 1):, :]
    return (y.reshape(b, t, d), c_new.reshape(b, t, -1), kpe_new.reshape(b, t, -1), s_new,
            new_conv, new_ffn)


def kernel(x_prompt, x_sample, cache_ckv, cache_kpe, page_table, state_gdn, state_gdn_conv,
           state_ffn_conv, w_in, gdn_conv_w, gdn_A_log, gdn_dt_bias, gdn_norm_w, mla_q_norm_w,
           mla_w_uq, mla_kv_norm_w, mla_w_uk, mla_w_uv, w_out, ln1_g, ln1_b, ffn_w_up, ffn_conv_w,
           ffn_w_down, ln2_g, ln2_b):
    depth = w_in.shape[0]
    alpha = (2.0 * depth) ** 0.25
    bp, tp, _ = x_prompt.shape
    bs, ts, _ = x_sample.shape
    assert ts == SUBLANES, "sample group: one 8-row tile per sequence"
    xp, xs = x_prompt, x_sample
    new_p, new_s = [], []
    for l in range(depth):
        wts = _prep_weights(w_in[l], gdn_conv_w[l], gdn_A_log[l], gdn_dt_bias[l], gdn_norm_w[l],
                            mla_q_norm_w[l], mla_w_uq[l], mla_kv_norm_w[l], mla_w_uk[l],
                            mla_w_uv[l], w_out[l], ln1_g[l], ln1_b[l], ffn_w_up[l], ffn_conv_w[l],
                            ffn_w_down[l], ln2_g[l], ln2_b[l])
        two_ff = ffn_w_up.shape[-1]
        xp, *st_p = _layer(
            xp, None, l, jnp.zeros((bp, GDN_H, GDN_D, GDN_D), F32),
            jnp.zeros((bp, GDN_CONV_TAPS - 1, GDN_QKV), F32),
            jnp.zeros((bp, FFN_CONV_TAPS - 1, two_ff), F32), wts, alpha)
        xs, *st_s = _layer(
            xs, (cache_ckv, cache_kpe, page_table), l, state_gdn[l], state_gdn_conv[l],
            state_ffn_conv[l], wts, alpha)
        new_p.append(st_p)
        new_s.append(st_s)
    stack = lambda sts: [jnp.stack(v) for v in zip(*sts)]
    return (xp, xs, *stack(new_p), *stack(new_s))
```

```python
import functools
import math

import jax
import jax.numpy as jnp
import numpy as np
from jax import lax
from jax.experimental import pallas as pl
from jax.experimental.pallas import tpu as pltpu

F32 = jnp.float32
BF16 = jnp.bfloat16

GDN_H = 8
GDN_D = 64
GDN_QK = GDN_H * GDN_D
GDN_QKV = 3 * GDN_QK
GDN_CONV_TAPS = 4
GDN_CHUNK = 64
GDN_GROUP = 256
GDN_GROUPS = GDN_QK // GDN_GROUP
HEADS_PER_GROUP = GDN_GROUP // GDN_D
MLA_H = 8
MLA_Q_RANK = 256
MLA_KV_RANK = 128
MLA_NOPE_D = 64
MLA_ROPE_D = 32
MLA_V_D = 64
MLA_QK = MLA_KV_RANK + MLA_ROPE_D
MLA_SOFTMAX_SCALE = (MLA_NOPE_D + MLA_ROPE_D) ** -0.5
LOG2_E = math.log2(math.e)
CT_ROWS = MLA_KV_RANK + 16
ROPE_BASE = 10000.0
FFN_CONV_TAPS = 3
RMS_EPS = 1e-6
LN_EPS = 1e-5
L2_EPS = 1e-6

SUBLANES = 8
LANES = 128
VMEM_LIMIT_BYTES = 56 * 1024 * 1024
FFN_CHUNK = 256
CONV_COLS = 256
SAMPLE_PAGES_PER_STEP = 64

_NT = (((1,), (1,)), ((), ()))
_TN = (((0,), (0,)), ((), ()))


def _mm(a, b):
    return jnp.dot(a.astype(BF16), b.astype(BF16), preferred_element_type=F32)


def _mm_nt(a, b):
    return lax.dot_general(a.astype(BF16), b.astype(BF16), _NT, preferred_element_type=F32)


def _sigmoid(x):
    return 1.0 / (1.0 + jnp.exp(-x))


def _silu(x):
    return x * _sigmoid(x)


def _rms(x, w):
    return x * lax.rsqrt(jnp.mean(x * x, axis=-1, keepdims=True) + RMS_EPS) * w


def _layernorm(x, g, b):
    mu = jnp.mean(x, axis=-1, keepdims=True)
    xc = x - mu
    var = jnp.mean(xc * xc, axis=-1, keepdims=True)
    return xc * lax.rsqrt(var + LN_EPS) * g + b


def _eye(n, dtype):
    return (lax.broadcasted_iota(jnp.int32, (n, n), 0)
            == lax.broadcasted_iota(jnp.int32, (n, n), 1)).astype(dtype)


def _const_spec(shape):
    nd = len(shape)
    return pl.BlockSpec(shape, lambda *_: (0,) * nd, pipeline_mode=pl.Buffered(1))


def _in_proj_kernel(x_ref, w_qkv_ref, w_z_ref, w_rest_ref, w_q_ref, w_uk_ref, qn_ref, kvn_ref,
                    convw_ref, hist_ref, *refs, transposed_q, tiles_per_seq):
    if transposed_q:
        (cos_ref, sin_ref, cos_k_ref, sin_k_ref,
         qkv_ref, tail_ref, z_ref, small_ref, ckv_ref, kpe_ref, q_ref, k_ref, ct_ref, xbuf) = refs
    else:
        cos_ref, sin_ref, qkv_ref, tail_ref, z_ref, small_ref, ckv_ref, kpe_ref, q_ref, k_ref = refs
    tm = x_ref.shape[0]
    xb = x_ref[...].astype(BF16)
    if transposed_q:
        @pl.when(pl.program_id(0) % tiles_per_seq == 0)
        def _():
            xbuf[0:SUBLANES, :] = hist_ref[0]
    else:
        nb = tm // SUBLANES
        t_pos = lax.broadcasted_iota(jnp.int32, (nb, SUBLANES, CONV_COLS), 1)
    for c0 in range(0, GDN_QKV, CONV_COLS):
        cols = slice(c0, c0 + CONV_COLS)
        raw = jnp.dot(xb, w_qkv_ref[:, cols], preferred_element_type=F32)
        cw = convw_ref[:, cols]
        if transposed_q:
            xbuf[SUBLANES:SUBLANES + tm, cols] = raw
            y = raw * cw[GDN_CONV_TAPS - 1:GDN_CONV_TAPS]
            for j in range(GDN_CONV_TAPS - 1):
                y = y + xbuf[pl.ds(SUBLANES - (GDN_CONV_TAPS - 1) + j, tm), cols] * cw[j:j + 1]
            tail = xbuf[tm:tm + SUBLANES, cols]
            xbuf[0:SUBLANES, cols] = tail
            tail_ref[0, :, cols] = tail
        else:
            raw3 = raw.reshape(nb, SUBLANES, CONV_COLS)
            y = raw3 * cw[GDN_CONV_TAPS - 1:GDN_CONV_TAPS]
            for back in range(1, GDN_CONV_TAPS):
                prev = pltpu.roll(raw3, back, axis=1)
                for t_row in range(back):
                    h_row = SUBLANES - back + t_row
                    prev = jnp.where(t_pos == t_row, hist_ref[:, h_row:h_row + 1, cols], prev)
                y = y + prev * cw[GDN_CONV_TAPS - 1 - back:GDN_CONV_TAPS - back]
            y = y.reshape(tm, CONV_COLS)
            tail_ref[:, :, cols] = raw3
        qkv_ref[:, cols] = _silu(y)
    z_ref[...] = jnp.dot(xb, w_z_ref[...], preferred_element_type=F32)
    rest = jnp.dot(xb, w_rest_ref[...], preferred_element_type=F32)
    cq = rest[:, 0:MLA_Q_RANK]
    ckv_raw = rest[:, MLA_Q_RANK:MLA_Q_RANK + MLA_KV_RANK]
    small = rest[:, MLA_Q_RANK + MLA_KV_RANK:]
    small_ref[...] = small

    c_new = _rms(ckv_raw, kvn_ref[...])
    ckv_ref[...] = c_new
    cqn = _rms(cq, qn_ref[...])
    n_nope = MLA_H * MLA_NOPE_D
    npe = MLA_H * MLA_ROPE_D
    k_rot_a = small[:, 16:16 + MLA_ROPE_D]
    k_rot_b = small[:, 16 + MLA_ROPE_D:16 + 2 * MLA_ROPE_D]

    if transposed_q:
        cos_t = cos_ref[...]
        sin_t = sin_ref[...]
        tm = x_ref.shape[0]
        kpe = k_rot_a * cos_k_ref[...] + k_rot_b * sin_k_ref[...]
        ct_ref[0:MLA_KV_RANK, :] = lax.dot_general(_eye(LANES, BF16), c_new.astype(BF16), _NT,
                                                   preferred_element_type=F32).astype(ct_ref.dtype)
        ct_ref[MLA_KV_RANK:, :] = jnp.ones((CT_ROWS - MLA_KV_RANK, tm), ct_ref.dtype)
        q_t = lax.dot_general(w_q_ref[...], cqn.astype(BF16), _NT,
                              preferred_element_type=F32)
        scale = MLA_SOFTMAX_SCALE * LOG2_E
        q_pe = (q_t[n_nope:n_nope + npe] * cos_t + q_t[n_nope + npe:] * sin_t) * scale
        for h in range(MLA_H):
            q_lat = _mm(w_uk_ref[h], q_t[h * MLA_NOPE_D:(h + 1) * MLA_NOPE_D]) * scale
            cols = slice(h * tm, (h + 1) * tm)
            q_ref[0:MLA_KV_RANK, cols] = q_lat.astype(q_ref.dtype)
            q_ref[MLA_KV_RANK:MLA_QK, cols] = q_pe[h * MLA_ROPE_D:(h + 1) * MLA_ROPE_D].astype(q_ref.dtype)
    else:
        cos = cos_ref[...]
        sin = sin_ref[...]
        kpe = k_rot_a * cos[:, 0:MLA_ROPE_D] + k_rot_b * sin[:, 0:MLA_ROPE_D]
        q = _mm(cqn, w_q_ref[...])
        q_pe = (q[:, n_nope:n_nope + npe] * cos + q[:, n_nope + npe:] * sin) * MLA_SOFTMAX_SCALE
        for h in range(MLA_H):
            q_lat = _mm(q[:, h * MLA_NOPE_D:(h + 1) * MLA_NOPE_D], w_uk_ref[h]) * MLA_SOFTMAX_SCALE
            q_ref[h, :, 0:MLA_KV_RANK] = q_lat.astype(q_ref.dtype)
            q_ref[h, :, MLA_KV_RANK:MLA_QK] = q_pe[:, h * MLA_ROPE_D:(h + 1) * MLA_ROPE_D].astype(q_ref.dtype)

    kpe_ref[...] = kpe
    k_ref[:, 0:MLA_KV_RANK] = c_new.astype(k_ref.dtype)
    k_ref[:, MLA_KV_RANK:MLA_QK] = kpe.astype(k_ref.dtype)


def _in_proj(x2, hist8, wts, rope, tm, t, transposed_q):
    m, d = x2.shape
    b = m // t
    row = lambda i: (i, 0)
    npe = MLA_H * MLA_ROPE_D
    act_dtype = BF16 if transposed_q else F32
    if transposed_q:
        tiles_per_seq = t // tm
        seq_spec = pl.BlockSpec((1, SUBLANES, GDN_QKV), lambda i: (i // tiles_per_seq, 0, 0))
        scratch = [pltpu.VMEM((tm + SUBLANES, GDN_QKV), F32)]
    else:
        tiles_per_seq = 1
        seq_spec = pl.BlockSpec((tm // SUBLANES, SUBLANES, GDN_QKV), lambda i: (i, 0, 0))
        scratch = []
    out_shape = [
        jax.ShapeDtypeStruct((m, GDN_QKV), F32),
        jax.ShapeDtypeStruct((b, SUBLANES, GDN_QKV), F32),
        jax.ShapeDtypeStruct((m, GDN_QK), F32),
        jax.ShapeDtypeStruct((m, LANES), F32),
        jax.ShapeDtypeStruct((m, MLA_KV_RANK), F32),
        jax.ShapeDtypeStruct((m, MLA_ROPE_D), F32),
    ]
    out_specs = [
        pl.BlockSpec((tm, GDN_QKV), row), seq_spec, pl.BlockSpec((tm, GDN_QK), row),
        pl.BlockSpec((tm, LANES), row), pl.BlockSpec((tm, MLA_KV_RANK), row),
        pl.BlockSpec((tm, MLA_ROPE_D), row),
    ]
    if transposed_q:
        n_rope_tiles = rope[0].shape[1] // tm
        rope_specs = [pl.BlockSpec((npe, tm), lambda i: (0, i % n_rope_tiles))] * 2
        rope_specs += [pl.BlockSpec((tm, MLA_ROPE_D), lambda i: (i % n_rope_tiles, 0))] * 2
        w_q, w_uk = wts["w_q_t"], wts["w_uk"]
        out_shape += [jax.ShapeDtypeStruct((m // tm, MLA_QK, MLA_H * tm), act_dtype),
                      jax.ShapeDtypeStruct((m, MLA_QK), act_dtype),
                      jax.ShapeDtypeStruct((CT_ROWS, m), act_dtype)]
        out_specs += [pl.BlockSpec((None, MLA_QK, MLA_H * tm), lambda i: (i, 0, 0)),
                      pl.BlockSpec((tm, MLA_QK), row),
                      pl.BlockSpec((CT_ROWS, tm), lambda i: (0, i))]
    else:
        n_rope_tiles = rope[0].shape[0] // tm
        rope_specs = [pl.BlockSpec((tm, npe), lambda i: (i % n_rope_tiles, 0))] * 2
        w_q, w_uk = wts["w_q"], wts["w_uk_t"]
        out_shape += [jax.ShapeDtypeStruct((MLA_H, m, MLA_QK), act_dtype),
                      jax.ShapeDtypeStruct((m, MLA_QK), act_dtype)]
        out_specs += [pl.BlockSpec((MLA_H, tm, MLA_QK), lambda i: (0, i, 0)),
                      pl.BlockSpec((tm, MLA_QK), row)]
    return pl.pallas_call(
        functools.partial(_in_proj_kernel, transposed_q=transposed_q, tiles_per_seq=tiles_per_seq),
        grid=(m // tm,),
        in_specs=[
            pl.BlockSpec((tm, d), row),
            _const_spec(wts["w_qkv"].shape), _const_spec(wts["w_z"].shape),
            _const_spec(wts["w_rest"].shape), _const_spec(w_q.shape), _const_spec(w_uk.shape),
            _const_spec(wts["q_norm"].shape), _const_spec(wts["kv_norm"].shape),
            _const_spec(wts["gdn_conv_w"].shape), seq_spec,
            *rope_specs,
        ],
        out_specs=tuple(out_specs),
        out_shape=tuple(out_shape),
        scratch_shapes=scratch,
        compiler_params=pltpu.CompilerParams(
            dimension_semantics=("arbitrary",), vmem_limit_bytes=VMEM_LIMIT_BYTES),
        name="in_proj",
    )(x2, wts["w_qkv"], wts["w_z"], wts["w_rest"], w_q, w_uk, wts["q_norm"], wts["kv_norm"],
      wts["gdn_conv_w"], hist8, *rope)


def _gdn_kernel(qkv_ref, z_ref, small_ref, s0_ref, gpar_ref, normw_ref,
                tril_ref, bones_ref, eb_ref, eg_ref,
                o_ref, sout_ref, s_scr, o_buf, *, t_in, tt):
    t = pl.program_id(1)
    c_rows, gw = GDN_CHUNK, GDN_GROUP
    padded = tt > t_in

    @pl.when(t == 0)
    def _():
        s_scr[...] = jnp.zeros(s_scr.shape, F32)
        for h in range(GDN_H):
            g, hh = divmod(h, HEADS_PER_GROUP)
            s_scr[g, hh * GDN_D:(hh + 1) * GDN_D, hh * GDN_D:(hh + 1) * GDN_D] = s0_ref[0, h]

    qkv = qkv_ref[0]
    small = small_ref[0]
    z = z_ref[0]
    if padded:
        row_valid = lax.broadcasted_iota(jnp.int32, (tt, 1), 0) < t_in
        qkv = jnp.concatenate([qkv, jnp.zeros((tt - t_in, GDN_QKV), F32)], axis=0)
        small = jnp.concatenate([small, jnp.zeros((tt - t_in, LANES), F32)], axis=0)
        z = jnp.concatenate([z, jnp.zeros((tt - t_in, GDN_QK), F32)], axis=0)
    beta_all = _sigmoid(small)
    xa = small + gpar_ref[1:2, :]
    softplus = jnp.maximum(xa, 0.0) + jnp.log1p(jnp.exp(-jnp.abs(xa)))
    g_all = -jnp.exp(gpar_ref[0:1, :]) * softplus
    if padded:
        beta_all = jnp.where(row_valid, beta_all, 0.0)
        g_all = jnp.where(row_valid, g_all, 0.0)

    def split3(x):
        hi = x.astype(BF16)
        r1 = x - hi.astype(F32)
        mid = r1.astype(BF16)
        return hi, mid, (r1 - mid.astype(F32)).astype(BF16)

    def mm_exact(x, ones_mat, terms=3):
        return sum(jnp.dot(part, ones_mat, preferred_element_type=F32) for part in split3(x)[:terms])

    def mm_exact_left(ones_mat, x):
        return sum(jnp.dot(ones_mat, part, preferred_element_type=F32) for part in split3(x))

    gc_all = mm_exact_left(tril_ref[...], g_all)
    beta_x = mm_exact(beta_all, eb_ref[...])
    gc_x = mm_exact(gc_all, eg_ref[...])
    block_ones = bones_ref[...]

    def block_diag(x):
        return jnp.concatenate([x.astype(BF16)] * HEADS_PER_GROUP, axis=0) * block_ones

    rr = lax.broadcasted_iota(jnp.int32, (c_rows, gw), 0)
    cc = lax.broadcasted_iota(jnp.int32, (c_rows, gw), 1) % c_rows
    incl, strict, diag = rr >= cc, rr > cc, rr == cc
    eye_t = diag.astype(F32)
    n_levels = int(math.log2(c_rows)) - 1
    n_chunks = tt // c_rows

    q_n, k_n = [], []
    for g in range(GDN_GROUPS):
        q_g = qkv[:, g * gw:(g + 1) * gw]
        k_g = qkv[:, GDN_QK + g * gw:GDN_QK + (g + 1) * gw]
        q_n.append(q_g * lax.rsqrt(mm_exact(q_g * q_g, block_ones, 2) + L2_EPS) * (GDN_D ** -0.5))
        k_n.append(k_g * lax.rsqrt(mm_exact(k_g * k_g, block_ones, 2) + L2_EPS))

    units = [(g, c) for c in range(n_chunks) for g in range(GDN_GROUPS)]
    st = {}
    for g, c in units:
        rows = slice(c * c_rows, (c + 1) * c_rows)
        lanes = slice(g * gw, (g + 1) * gw)
        q_c, k_c = q_n[g][rows], k_n[g][rows]
        v_c = qkv[rows, 2 * GDN_QK + g * gw:2 * GDN_QK + (g + 1) * gw]
        b_c = beta_x[rows, lanes]
        gc_c = gc_x[rows, lanes]
        g_row = jnp.sum(jnp.where(diag, gc_c, 0.0), axis=0, keepdims=True)
        g_last = gc_c[c_rows - 1:c_rows, :]
        decay = jnp.exp(jnp.where(incl, gc_c - g_row, -jnp.inf))
        e_col = jnp.exp(gc_c)
        kb = k_c * b_c
        kk_qk = lax.dot_general(jnp.concatenate([kb, q_c], axis=0).astype(BF16), block_diag(k_c),
                                _NT, preferred_element_type=F32)
        lmat = jnp.where(strict, kk_qk[0:c_rows] * decay, 0.0)
        st[g, c] = dict(
            inv=eye_t - lmat, lmat=lmat,
            a=(kk_qk[c_rows:] * decay).astype(BF16),
            vb=block_diag(v_c * b_c), kbe=block_diag(kb * e_col),
            qg=q_c * e_col,
            kg=(k_c * jnp.exp(g_last - gc_c)).astype(BF16),
            s_decay=jnp.exp(g_last),
        )
    for u in units:
        st[u]["pw"] = _mm(st[u]["lmat"], block_diag(st[u]["lmat"]))
    for lvl in range(n_levels):
        for u in units:
            s_u = st[u]
            bd = block_diag(s_u["pw"])
            if lvl < n_levels - 1:
                both = _mm(jnp.concatenate([s_u["pw"], s_u["inv"]], axis=0), bd)
                s_u["pw"] = both[0:c_rows]
                s_u["inv"] = s_u["inv"] + both[c_rows:]
            else:
                s_u["inv"] = s_u["inv"] + _mm(s_u["inv"], bd)
    for u in units:
        inv_b = st[u]["inv"].astype(BF16)
        st[u]["u"] = jnp.dot(inv_b, st[u]["vb"], preferred_element_type=F32)
        st[u]["w"] = jnp.dot(inv_b, st[u]["kbe"], preferred_element_type=F32)

    s_bd = [s_scr[g] for g in range(GDN_GROUPS)]
    block_ones_f = block_ones.astype(F32)
    for g, c in units:
        p = st[g, c]
        both = _mm(jnp.concatenate([p["w"], p["qg"]], axis=0), s_bd[g])
        v_new = p["u"] - both[0:c_rows]
        o_buf[c * c_rows:(c + 1) * c_rows, g * gw:(g + 1) * gw] = (
            both[c_rows:] + jnp.dot(p["a"], block_diag(v_new), preferred_element_type=F32))
        upd = lax.dot_general(p["kg"], v_new.astype(BF16), _TN, preferred_element_type=F32)
        s_bd[g] = s_bd[g] * p["s_decay"] + upd * block_ones_f
    for g in range(GDN_GROUPS):
        s_scr[g] = s_bd[g]

    norm_w = jnp.concatenate([normw_ref[...]] * HEADS_PER_GROUP, axis=1)
    outs = []
    for g in range(GDN_GROUPS):
        o_g = o_buf[:, g * gw:(g + 1) * gw]
        ms = mm_exact(o_g * o_g, block_ones, 2) * (1.0 / GDN_D)
        outs.append(o_g * lax.rsqrt(ms + RMS_EPS) * norm_w * _silu(z[:, g * gw:(g + 1) * gw]))
    o_ref[0] = jnp.concatenate(outs, axis=1)[0:t_in].astype(o_ref.dtype)

    @pl.when(t == pl.num_programs(1) - 1)
    def _():
        for h in range(GDN_H):
            g, hh = divmod(h, HEADS_PER_GROUP)
            sout_ref[0, h] = s_scr[g, hh * GDN_D:(hh + 1) * GDN_D, hh * GDN_D:(hh + 1) * GDN_D]


def _gdn_constants(tt):
    tril = np.kron(np.eye(tt // GDN_CHUNK), np.tril(np.ones((GDN_CHUNK, GDN_CHUNK))))
    block_ones = np.kron(np.eye(HEADS_PER_GROUP), np.ones((GDN_D, GDN_D)))
    spread = np.kron(np.eye(GDN_H), np.ones((1, GDN_D)))
    e_b = np.zeros((LANES, GDN_QK))
    e_g = np.zeros((LANES, GDN_QK))
    e_b[0:GDN_H] = spread
    e_g[GDN_H:2 * GDN_H] = spread
    return [jnp.asarray(a, BF16) for a in (tril, block_ones, e_b, e_g)]


def _gdn(qkv, z, small, s0, gpar, norm_w, t_in, act_dtype):
    b, t, _ = qkv.shape
    tt = max(t_in, GDN_CHUNK)
    consts = _gdn_constants(tt)
    seq = lambda i, j: (i, j, 0)
    per_b4 = lambda i, j: (i, 0, 0, 0)
    return pl.pallas_call(
        functools.partial(_gdn_kernel, t_in=t_in, tt=tt),
        grid=(b, t // t_in),
        in_specs=[
            pl.BlockSpec((1, t_in, GDN_QKV), seq), pl.BlockSpec((1, t_in, GDN_QK), seq),
            pl.BlockSpec((1, t_in, LANES), seq),
            pl.BlockSpec((1, GDN_H, GDN_D, GDN_D), per_b4),
            _const_spec(gpar.shape), _const_spec(norm_w.shape),
            *[_const_spec(c.shape) for c in consts],
        ],
        out_specs=(pl.BlockSpec((1, t_in, GDN_QK), seq),
                   pl.BlockSpec((1, GDN_H, GDN_D, GDN_D), per_b4)),
        out_shape=(jax.ShapeDtypeStruct((b, t, GDN_QK), act_dtype),
                   jax.ShapeDtypeStruct((b, GDN_H, GDN_D, GDN_D), F32)),
        scratch_shapes=[pltpu.VMEM((GDN_GROUPS, GDN_GROUP, GDN_GROUP), F32),
                        pltpu.VMEM((tt, GDN_QK), F32)],
        compiler_params=pltpu.CompilerParams(
            dimension_semantics=("parallel", "arbitrary"), vmem_limit_bytes=VMEM_LIMIT_BYTES),
        name="gdn",
    )(qkv, z, small, s0, gpar, norm_w, *consts)


def _attn_prompt_kernel(qi_ref, kj_ref, qt_ref, k_ref, ct_ref, w_uvt_ref, o_ref, m_scr, acc_scr,
                        *, tq, tk):
    p = pl.program_id(1)
    i = qi_ref[p]
    j = kj_ref[p]
    kv_per_q = tq // tk
    first_diag = i * kv_per_q

    @pl.when(j == 0)
    def _():
        m_scr[...] = jnp.full(m_scr.shape, -jnp.inf, F32)
        acc_scr[...] = jnp.zeros(acc_scr.shape, F32)

    def update(masked):
        s = jnp.dot(k_ref[...], qt_ref[...], preferred_element_type=F32)
        if masked:
            key = lax.broadcasted_iota(jnp.int32, s.shape, 0) + (j - first_diag) * tk
            qry = lax.broadcasted_iota(jnp.int32, s.shape, 1) % tq
            s = jnp.where(key <= qry, s, -jnp.inf)
        m_prev = m_scr[...]
        m_new = jnp.maximum(m_prev, jnp.max(s, axis=0, keepdims=True))
        alpha = jnp.exp2(m_prev - m_new)
        pmat = jnp.exp2(s - m_new)
        acc_scr[...] = alpha * acc_scr[...] + jnp.dot(ct_ref[...], pmat.astype(BF16),
                                                      preferred_element_type=F32)
        m_scr[...] = m_new

    @pl.when(j < first_diag)
    def _():
        update(False)

    @pl.when(j >= first_diag)
    def _():
        update(True)

    @pl.when(j == first_diag + kv_per_q - 1)
    def _():
        o_lat_t = acc_scr[0:MLA_KV_RANK, :] / acc_scr[MLA_KV_RANK:MLA_KV_RANK + 1, :]
        outs = [_mm(w_uvt_ref[h], o_lat_t[:, h * tq:(h + 1) * tq]) for h in range(MLA_H)]
        o_ref[...] = jnp.concatenate(outs, axis=0).T.astype(o_ref.dtype)


def _attn_prompt(q_t, k, c_t, w_uv_t, b, t, tq, tk):
    nq, nk = t // tq, t // tk
    kv_per_q = tq // tk
    m = b * t
    assert q_t.shape == (m // tq, MLA_QK, MLA_H * tq), "query tiles must match the projection tiles"
    pairs = [(i, j) for i in range(nq) for j in range((i + 1) * kv_per_q)]
    qi = jnp.asarray(np.array([p[0] for p in pairs], np.int32))
    kj = jnp.asarray(np.array([p[1] for p in pairs], np.int32))
    grid_spec = pltpu.PrefetchScalarGridSpec(
        num_scalar_prefetch=2,
        grid=(b, len(pairs)),
        in_specs=[
            pl.BlockSpec((None, MLA_QK, MLA_H * tq), lambda bi, p, qi, kj: (bi * nq + qi[p], 0, 0)),
            pl.BlockSpec((tk, MLA_QK), lambda bi, p, qi, kj: (bi * nk + kj[p], 0)),
            pl.BlockSpec((CT_ROWS, tk), lambda bi, p, qi, kj: (0, bi * nk + kj[p])),
            pl.BlockSpec(w_uv_t.shape, lambda bi, p, qi, kj: (0, 0, 0)),
        ],
        out_specs=pl.BlockSpec((tq, MLA_H * MLA_V_D), lambda bi, p, qi, kj: (bi * nq + qi[p], 0)),
        scratch_shapes=[pltpu.VMEM((1, MLA_H * tq), F32),
                        pltpu.VMEM((CT_ROWS, MLA_H * tq), F32)],
    )
    return pl.pallas_call(
        functools.partial(_attn_prompt_kernel, tq=tq, tk=tk),
        grid_spec=grid_spec,
        out_shape=jax.ShapeDtypeStruct((m, MLA_H * MLA_V_D), BF16),
        compiler_params=pltpu.CompilerParams(
            dimension_semantics=("parallel", "arbitrary"), vmem_limit_bytes=VMEM_LIMIT_BYTES),
        name="attn_prompt",
    )(qi, kj, q_t, k, c_t, w_uv_t)


def _softmax_update(s, v_b, m_scr, l_scr, acc_scr):
    m_prev = m_scr[...]
    m_new = jnp.maximum(m_prev, jnp.max(s, axis=-1, keepdims=True))
    alpha = jnp.exp(m_prev - m_new)
    p = jnp.exp(s - m_new)
    l_scr[...] = alpha * l_scr[...] + jnp.sum(p, axis=-1, keepdims=True)
    acc_scr[...] = alpha * acc_scr[...] + jnp.dot(p.astype(BF16), v_b, preferred_element_type=F32)
    m_scr[...] = m_new


def _attn_sample_kernel(pt_ref, q_ref, knew_ref, w_uv_ref, ckv_hbm, kpe_hbm, o_ref,
                        c_buf, kpe_buf, sems, m_scr, l_scr, acc_scr, *,
                        layer, pages_per_step, page, t_new):
    seq = pl.program_id(0)
    grp = pl.program_id(1)
    n_grp = pl.num_programs(1)
    step = seq * n_grp + grp
    slot = step % 2
    rows = MLA_H * t_new

    def page_copies(seq_i, grp_i, slot_i, p):
        pid = pt_ref[seq_i, grp_i * pages_per_step + p]
        return (
            pltpu.make_async_copy(ckv_hbm.at[layer, pid],
                                  c_buf.at[slot_i, pl.ds(p * page, page), :], sems.at[0, slot_i]),
            pltpu.make_async_copy(kpe_hbm.at[layer, pid],
                                  kpe_buf.at[slot_i, :, pl.ds(p * page, page)], sems.at[1, slot_i]),
        )

    def start_fetch(seq_i, grp_i, slot_i):
        for p in range(pages_per_step):
            for cp in page_copies(seq_i, grp_i, slot_i, p):
                cp.start()

    @pl.when(step == 0)
    def _():
        start_fetch(seq, grp, slot)

    @pl.when(step + 1 < pl.num_programs(0) * n_grp)
    def _():
        nxt = step + 1
        start_fetch(nxt // n_grp, nxt % n_grp, 1 - slot)

    @pl.when(grp == 0)
    def _():
        m_scr[...] = jnp.full(m_scr.shape, -jnp.inf, F32)
        l_scr[...] = jnp.zeros(l_scr.shape, F32)
        acc_scr[...] = jnp.zeros(acc_scr.shape, F32)

    for p in range(pages_per_step):
        for cp in page_copies(seq, grp, slot, p):
            cp.wait()

    q = q_ref[...].reshape(rows, MLA_QK).astype(BF16)
    c_b = c_buf[slot].astype(BF16)
    s = (lax.dot_general(q[:, 0:MLA_KV_RANK], c_b, _NT, preferred_element_type=F32)
         + jnp.dot(q[:, MLA_KV_RANK:MLA_QK], kpe_buf[slot].astype(BF16), preferred_element_type=F32))
    _softmax_update(s, c_b, m_scr, l_scr, acc_scr)

    @pl.when(grp == n_grp - 1)
    def _():
        k_new = knew_ref[...]
        s = _mm_nt(q, k_new)
        qi = lax.broadcasted_iota(jnp.int32, (rows, t_new), 0) % t_new
        ki = lax.broadcasted_iota(jnp.int32, (rows, t_new), 1)
        s = jnp.where(ki <= qi, s, -jnp.inf)
        _softmax_update(s, k_new[:, 0:MLA_KV_RANK].astype(BF16), m_scr, l_scr, acc_scr)
        o_lat = acc_scr[...] / l_scr[...]
        outs = [_mm(o_lat[h * t_new:(h + 1) * t_new], w_uv_ref[h]) for h in range(MLA_H)]
        o_ref[...] = jnp.concatenate(outs, axis=-1).astype(o_ref.dtype)


def _attn_sample(q, k_new, cache_ckv, cache_kpe, layer, page_table, w_uv, b, t_new):
    cache_kpe = jnp.swapaxes(cache_kpe, 2, 3)
    n_pages = page_table.shape[1]
    page = cache_ckv.shape[2]
    pages_per_step = math.gcd(SAMPLE_PAGES_PER_STEP, n_pages)
    rows = MLA_H * t_new
    n_steps = n_pages // pages_per_step

    keys = pages_per_step * page
    grid_spec = pltpu.PrefetchScalarGridSpec(
        num_scalar_prefetch=1,
        grid=(b, n_steps),
        in_specs=[
            pl.BlockSpec((MLA_H, t_new, MLA_QK), lambda bi, s, pt: (0, bi, 0)),
            pl.BlockSpec((t_new, MLA_QK), lambda bi, s, pt: (bi, 0)),
            pl.BlockSpec(w_uv.shape, lambda bi, s, pt: (0, 0, 0)),
            pl.BlockSpec(memory_space=pl.ANY),
            pl.BlockSpec(memory_space=pl.ANY),
        ],
        out_specs=pl.BlockSpec((t_new, MLA_H * MLA_V_D), lambda bi, s, pt: (bi, 0)),
        scratch_shapes=[pltpu.VMEM((2, keys, MLA_KV_RANK), F32),
                        pltpu.VMEM((2, MLA_ROPE_D, keys), F32),
                        pltpu.SemaphoreType.DMA((2, 2)),
                        pltpu.VMEM((rows, 1), F32), pltpu.VMEM((rows, 1), F32),
                        pltpu.VMEM((rows, MLA_KV_RANK), F32)],
    )
    return pl.pallas_call(
        functools.partial(_attn_sample_kernel, layer=layer, pages_per_step=pages_per_step,
                          page=page, t_new=t_new),
        grid_spec=grid_spec,
        out_shape=jax.ShapeDtypeStruct((b * t_new, MLA_H * MLA_V_D), F32),
        compiler_params=pltpu.CompilerParams(
            dimension_semantics=("arbitrary", "arbitrary"), vmem_limit_bytes=VMEM_LIMIT_BYTES),
        name="attn_sample",
    )(page_table, q, k_new, w_uv, cache_ckv, cache_kpe)


def _ffn_kernel(x_ref, og_ref, om_ref, hist_ref, w_out_ref, ln_ref, w_up_ref, cw_ref, w_dn_ref,
                y_ref, nf_ref, *scratch, tm, alpha, short_seq):
    d_gdn = og_ref.shape[-1]
    n_chunks, _, two_fc = w_up_ref.shape
    fc = two_fc // 2
    mix = _mm(og_ref[...], w_out_ref[0:d_gdn, :]) + _mm(om_ref[...], w_out_ref[d_gdn:, :])
    x1 = _layernorm(alpha * x_ref[...] + mix, ln_ref[0:1, :], ln_ref[1:2, :])
    x1_b = x1.astype(BF16)

    if short_seq:
        h_buf, = scratch
        nb = tm // SUBLANES
        t_pos = lax.broadcasted_iota(jnp.int32, (nb, SUBLANES, two_fc), 1)
    else:
        h_buf, ubuf, carry = scratch

        @pl.when(pl.program_id(1) == 0)
        def _():
            carry[...] = hist_ref[0]

    for c in range(n_chunks):
        cols = slice(c * two_fc, (c + 1) * two_fc)
        up = jnp.dot(x1_b, w_up_ref[c], preferred_element_type=F32)
        cw = cw_ref[c]
        if short_seq:
            up3 = up.reshape(nb, SUBLANES, two_fc)
            h0 = hist_ref[:, SUBLANES - 2:SUBLANES - 1, cols]
            h1 = hist_ref[:, SUBLANES - 1:SUBLANES, cols]
            prev1 = jnp.where(t_pos == 0, h1, pltpu.roll(up3, 1, axis=1))
            prev2 = jnp.where(t_pos == 0, h0, jnp.where(t_pos == 1, h1, pltpu.roll(up3, 2, axis=1)))
            hc = (prev2 * cw[0:1] + prev1 * cw[1:2] + up3 * cw[2:3]).reshape(tm, two_fc)
            nf_ref[:, :, cols] = up3
        else:
            ubuf[0:SUBLANES, :] = carry[:, cols]
            ubuf[SUBLANES:SUBLANES + tm, :] = up
            hc = (ubuf[pl.ds(SUBLANES - 2, tm), :] * cw[0:1]
                  + ubuf[pl.ds(SUBLANES - 1, tm), :] * cw[1:2] + up * cw[2:3])
            last = ubuf[tm:tm + SUBLANES, :]
            carry[:, cols] = last
            nf_ref[0, :, cols] = last
        h_buf[:, c * fc:(c + 1) * fc] = (_silu(hc[:, fc:]) * hc[:, 0:fc]).astype(BF16)
    down = jnp.dot(h_buf[...], w_dn_ref[...], preferred_element_type=F32)
    y_ref[...] = _layernorm(alpha * x1 + down, ln_ref[2:3, :], ln_ref[3:4, :])


def _ffn(x2, o_gdn, o_mla, hist8, wts, b, t, tm, alpha):
    m, d = x2.shape
    n_chunks, _, two_fc = wts["w_up"].shape
    two_ff = n_chunks * two_fc
    short_seq = t == SUBLANES
    if short_seq:
        grid = (m // tm, 1)
        nb = tm // SUBLANES
        row = lambda i, j: (i, 0)
        hist_spec = pl.BlockSpec((nb, SUBLANES, two_ff), lambda i, j: (i, 0, 0))
        scratch = [pltpu.VMEM((tm, two_ff // 2), BF16)]
    else:
        nt = t // tm
        grid = (b, nt)
        row = lambda i, j: (i * nt + j, 0)
        hist_spec = pl.BlockSpec((1, SUBLANES, two_ff), lambda i, j: (i, 0, 0))
        scratch = [pltpu.VMEM((tm, two_ff // 2), BF16),
                   pltpu.VMEM((tm + SUBLANES, two_fc), F32), pltpu.VMEM((SUBLANES, two_ff), F32)]
    return pl.pallas_call(
        functools.partial(_ffn_kernel, tm=tm, alpha=alpha, short_seq=short_seq),
        grid=grid,
        in_specs=[
            pl.BlockSpec((tm, d), row), pl.BlockSpec((tm, o_gdn.shape[1]), row),
            pl.BlockSpec((tm, o_mla.shape[1]), row), hist_spec,
            _const_spec(wts["w_out"].shape), _const_spec(wts["ln"].shape),
            _const_spec(wts["w_up"].shape), _const_spec(wts["ffn_cw"].shape),
            _const_spec(wts["w_dn"].shape),
        ],
        out_specs=(pl.BlockSpec((tm, d), row), hist_spec),
        out_shape=(jax.ShapeDtypeStruct((m, d), F32),
                   jax.ShapeDtypeStruct((b, SUBLANES, two_ff), F32)),
        scratch_shapes=scratch,
        compiler_params=pltpu.CompilerParams(
            dimension_semantics=("parallel", "arbitrary"), vmem_limit_bytes=VMEM_LIMIT_BYTES),
        name="ffn",
    )(x2, o_gdn, o_mla, hist8, wts["w_out"], wts["ln"], wts["w_up"], wts["ffn_cw"], wts["w_dn"])


def _swap_halves(w):
    half = w.shape[-1] // 2
    return jnp.concatenate([-w[..., half:], w[..., :half]], axis=-1)


def _prep_weights(w_in, gdn_conv_w, gdn_A_log, gdn_dt_bias, gdn_norm_w, mla_q_norm_w, mla_w_uq,
                  mla_kv_norm_w, mla_w_uk, mla_w_uv, w_out, ln1_g, ln1_b, ffn_w_up, ffn_conv_w,
                  ffn_w_down, ln2_g, ln2_b):
    d_model = w_in.shape[0]
    o = 0
    w_qkv = w_in[:, o:o + GDN_QKV]; o += GDN_QKV
    w_z = w_in[:, o:o + GDN_QK]; o += GDN_QK
    w_b = w_in[:, o:o + GDN_H]; o += GDN_H
    w_a = w_in[:, o:o + GDN_H]; o += GDN_H
    w_cq = w_in[:, o:o + MLA_Q_RANK]; o += MLA_Q_RANK
    w_ckv = w_in[:, o:o + MLA_KV_RANK]; o += MLA_KV_RANK
    w_kpe = w_in[:, o:o + MLA_ROPE_D]
    pad = jnp.zeros((d_model, LANES - 2 * GDN_H - 2 * MLA_ROPE_D), w_in.dtype)
    w_small = jnp.concatenate([w_b, w_a, w_kpe, _swap_halves(w_kpe), pad], axis=1)
    w_rest = jnp.concatenate([w_cq, w_ckv, w_small], axis=1)

    uq = mla_w_uq.reshape(MLA_Q_RANK, MLA_H, MLA_NOPE_D + MLA_ROPE_D)
    uq_nope = uq[:, :, :MLA_NOPE_D].reshape(MLA_Q_RANK, MLA_H * MLA_NOPE_D)
    uq_pe = uq[:, :, MLA_NOPE_D:]
    w_q = jnp.concatenate([uq_nope, uq_pe.reshape(MLA_Q_RANK, -1),
                           _swap_halves(uq_pe).reshape(MLA_Q_RANK, -1)], axis=1)

    gpar = jnp.zeros((2, LANES), F32)
    gpar = gpar.at[0, GDN_H:2 * GDN_H].set(gdn_A_log).at[1, GDN_H:2 * GDN_H].set(gdn_dt_bias)

    d_ff = ffn_w_down.shape[0]
    n_chunks = d_ff // FFN_CHUNK

    def interleave(w):
        lead = w.shape[:-1]
        w = w.reshape(lead + (2, n_chunks, FFN_CHUNK))
        return jnp.moveaxis(w, -3, -2).reshape(lead + (n_chunks, 2 * FFN_CHUNK))

    return {
        "w_qkv": w_qkv.astype(BF16), "w_z": w_z.astype(BF16), "w_rest": w_rest.astype(BF16),
        "w_q": w_q.astype(BF16), "w_q_t": w_q.T.astype(BF16),
        "w_uk_t": jnp.transpose(mla_w_uk, (1, 2, 0)).astype(BF16),
        "w_uk": jnp.transpose(mla_w_uk, (1, 0, 2)).astype(BF16),
        "w_uv": jnp.transpose(mla_w_uv, (1, 0, 2)).astype(BF16),
        "w_uv_t": jnp.transpose(mla_w_uv, (1, 2, 0)).astype(BF16),
        "q_norm": mla_q_norm_w.reshape(1, -1), "kv_norm": mla_kv_norm_w.reshape(1, -1),
        "gdn_conv_w": gdn_conv_w, "gpar": gpar, "gdn_norm_w": gdn_norm_w.reshape(1, -1),
        "w_out": w_out.astype(BF16),
        "ln": jnp.stack([ln1_g, ln1_b, ln2_g, ln2_b]),
        "w_up": jnp.transpose(interleave(ffn_w_up), (1, 0, 2)).astype(BF16),
        "ffn_cw": jnp.transpose(interleave(ffn_conv_w), (1, 0, 2)),
        "w_dn": ffn_w_down.astype(BF16),
        "interleave": interleave, "n_chunks": n_chunks,
    }


def _rope_tables(past_len, t, tm, transposed):
    half = MLA_ROPE_D // 2
    pos = (past_len + jnp.arange(t, dtype=jnp.int32)).astype(F32)
    inv = ROPE_BASE ** (-jnp.arange(half, dtype=F32) / half)
    ang = pos[:, None] * inv
    reps = (max(tm // t, 1), 2 * MLA_H)
    cos, sin = jnp.tile(jnp.cos(ang), reps), jnp.tile(jnp.sin(ang), reps)
    if transposed:
        return cos.T, sin.T, cos[:, 0:MLA_ROPE_D], sin[:, 0:MLA_ROPE_D]
    return cos, sin


def _pad_hist(h):
    return jnp.pad(h, ((0, 0), (SUBLANES - h.shape[1], 0), (0, 0)))


def _tiles(b, t):
    if t == SUBLANES:
        rows = min(256, b * t)
        return dict(proj=rows, ffn=rows, gdn=t, attn_q=None, attn_k=None)
    return dict(proj=min(512, t), ffn=min(512, t), gdn=min(512, t), attn_q=min(512, t),
                attn_k=min(256, t))


def _layer(x, past, layer, s0, conv_hist, ffn_hist, wts, alpha):
    b, t, d = x.shape
    m = b * t
    x2 = x.reshape(m, d)
    tiles = _tiles(b, t)
    short_seq = past is not None
    past_len = past[2].shape[1] * past[0].shape[2] if short_seq else 0
    act_dtype = F32 if short_seq else BF16
    rope = _rope_tables(past_len, t, tiles["proj"], transposed=not short_seq)
    qkv, conv_tail, z, small, c_new, kpe_new, q, k, *c_t = _in_proj(
        x2, _pad_hist(conv_hist), wts, rope, tiles["proj"], t, transposed_q=not short_seq)

    o_gdn, s_new = _gdn(qkv.reshape(b, t, -1), z.reshape(b, t, -1), small.reshape(b, t, -1),
                        s0, wts["gpar"], wts["gdn_norm_w"], tiles["gdn"], act_dtype)
    if short_seq:
        cache_ckv, cache_kpe, page_table = past
        o_mla = _attn_sample(q, k, cache_ckv, cache_kpe, layer, page_table, wts["w_uv"], b, t)
    else:
        o_mla = _attn_prompt(q, k, c_t[0], wts["w_uv_t"], b, t, tiles["attn_q"], tiles["attn_k"])

    hist8 = wts["interleave"](_pad_hist(ffn_hist)).reshape(b, SUBLANES, -1)
    y, nf8 = _ffn(x2, o_gdn.reshape(m, -1), o_mla, hist8, wts, b, t, tiles["ffn"], alpha)

    n_chunks = wts["n_chunks"]
    nf = nf8[:, SUBLANES - (FFN_CONV_TAPS - 1):, :].reshape(b, FFN_CONV_TAPS - 1, n_chunks, 2, FFN_CHUNK)
    new_ffn = jnp.moveaxis(nf, -2, -3).reshape(b, FFN_CONV_TAPS - 1, -1)
    new_conv = conv_tail[:, SUBLANES - (GDN_CONV_TAPS - 1):, :]
    return (y.reshape(b, t, d), c_new.reshape(b, t, -1), kpe_new.reshape(b, t, -1), s_new,
            new_conv, new_ffn)


def kernel(x_prompt, x_sample, cache_ckv, cache_kpe, page_table, state_gdn, state_gdn_conv,
           state_ffn_conv, w_in, gdn_conv_w, gdn_A_log, gdn_dt_bias, gdn_norm_w, mla_q_norm_w,
           mla_w_uq, mla_kv_norm_w, mla_w_uk, mla_w_uv, w_out, ln1_g, ln1_b, ffn_w_up, ffn_conv_w,
           ffn_w_down, ln2_g, ln2_b):
    depth = w_in.shape[0]
    alpha = (2.0 * depth) ** 0.25
    bp, tp, _ = x_prompt.shape
    bs, ts, _ = x_sample.shape
    assert ts == SUBLANES, "sample group: one 8-row tile per sequence"
    xp, xs = x_prompt, x_sample
    new_p, new_s = [], []
    for l in range(depth):
        wts = _prep_weights(w_in[l], gdn_conv_w[l], gdn_A_log[l], gdn_dt_bias[l], gdn_norm_w[l],
                            mla_q_norm_w[l], mla_w_uq[l], mla_kv_norm_w[l], mla_w_uk[l],
                            mla_w_uv[l], w_out[l], ln1_g[l], ln1_b[l], ffn_w_up[l], ffn_conv_w[l],
                            ffn_w_down[l], ln2_g[l], ln2_b[l])
        two_ff = ffn_w_up.shape[-1]
        xp, *st_p = _layer(
            xp, None, l, jnp.zeros((bp, GDN_H, GDN_D, GDN_D), F32),
            jnp.zeros((bp, GDN_CONV_TAPS - 1, GDN_QKV), F32),
            jnp.zeros((bp, FFN_CONV_TAPS - 1, two_ff), F32), wts, alpha)
        xs, *st_s = _layer(
            xs, (cache_ckv, cache_kpe, page_table), l, state_gdn[l], state_gdn_conv[l],
            state_ffn_conv[l], wts, alpha)
        new_p.append(st_p)
        new_s.append(st_s)
    stack = lambda sts: [jnp.stack(v) for v in zip(*sts)]
    return (xp, xs, *stack(new_p), *stack(new_s))
```

```python
import functools
import math

import jax
import jax.numpy as jnp
import numpy as np
from jax import lax
from jax.experimental import pallas as pl
from jax.experimental.pallas import tpu as pltpu

F32 = jnp.float32
BF16 = jnp.bfloat16

GDN_H = 8
GDN_D = 64
GDN_QK = GDN_H * GDN_D
GDN_QKV = 3 * GDN_QK
GDN_CONV_TAPS = 4
GDN_CHUNK = 64
GDN_GROUP = 256
GDN_GROUPS = GDN_QK // GDN_GROUP
HEADS_PER_GROUP = GDN_GROUP // GDN_D
MLA_H = 8
MLA_Q_RANK = 256
MLA_KV_RANK = 128
MLA_NOPE_D = 64
MLA_ROPE_D = 32
MLA_V_D = 64
MLA_QK = MLA_KV_RANK + MLA_ROPE_D
MLA_SOFTMAX_SCALE = (MLA_NOPE_D + MLA_ROPE_D) ** -0.5
LOG2_E = math.log2(math.e)
CT_ROWS = MLA_KV_RANK + 16
ROPE_BASE = 10000.0
FFN_CONV_TAPS = 3
RMS_EPS = 1e-6
LN_EPS = 1e-5
L2_EPS = 1e-6

SUBLANES = 8
LANES = 128
VMEM_LIMIT_BYTES = 56 * 1024 * 1024
FFN_CHUNK = 256
CONV_COLS = 256
SAMPLE_PAGES_PER_STEP = 128

_NT = (((1,), (1,)), ((), ()))
_TN = (((0,), (0,)), ((), ()))


def _mm(a, b):
    return jnp.dot(a.astype(BF16), b.astype(BF16), preferred_element_type=F32)


def _mm_nt(a, b):
    return lax.dot_general(a.astype(BF16), b.astype(BF16), _NT, preferred_element_type=F32)


def _sigmoid(x):
    return 1.0 / (1.0 + jnp.exp(-x))


def _silu(x):
    return x * _sigmoid(x)


def _rms(x, w):
    return x * lax.rsqrt(jnp.mean(x * x, axis=-1, keepdims=True) + RMS_EPS) * w


def _layernorm(x, g, b):
    mu = jnp.mean(x, axis=-1, keepdims=True)
    xc = x - mu
    var = jnp.mean(xc * xc, axis=-1, keepdims=True)
    return xc * lax.rsqrt(var + LN_EPS) * g + b


def _eye(n, dtype):
    return (lax.broadcasted_iota(jnp.int32, (n, n), 0)
            == lax.broadcasted_iota(jnp.int32, (n, n), 1)).astype(dtype)


def _const_spec(shape):
    nd = len(shape)
    return pl.BlockSpec(shape, lambda *_: (0,) * nd, pipeline_mode=pl.Buffered(1))


def _in_proj_kernel(x_ref, w_qkv_ref, w_z_ref, w_rest_ref, w_q_ref, w_uk_ref, qn_ref, kvn_ref,
                    convw_ref, hist_ref, *refs, transposed_q, tiles_per_seq):
    if transposed_q:
        (cos_ref, sin_ref, cos_k_ref, sin_k_ref,
         qkv_ref, tail_ref, z_ref, small_ref, ckv_ref, kpe_ref, q_ref, k_ref, ct_ref, xbuf) = refs
    else:
        cos_ref, sin_ref, qkv_ref, tail_ref, z_ref, small_ref, ckv_ref, kpe_ref, q_ref, k_ref = refs
    tm = x_ref.shape[0]
    xb = x_ref[...].astype(BF16)
    if transposed_q:
        @pl.when(pl.program_id(0) % tiles_per_seq == 0)
        def _():
            xbuf[0:SUBLANES, :] = hist_ref[0]
    else:
        nb = tm // SUBLANES
        t_pos = lax.broadcasted_iota(jnp.int32, (nb, SUBLANES, CONV_COLS), 1)
    for c0 in range(0, GDN_QKV, CONV_COLS):
        cols = slice(c0, c0 + CONV_COLS)
        raw = jnp.dot(xb, w_qkv_ref[:, cols], preferred_element_type=F32)
        cw = convw_ref[:, cols]
        if transposed_q:
            xbuf[SUBLANES:SUBLANES + tm, cols] = raw
            y = raw * cw[GDN_CONV_TAPS - 1:GDN_CONV_TAPS]
            for j in range(GDN_CONV_TAPS - 1):
                y = y + xbuf[pl.ds(SUBLANES - (GDN_CONV_TAPS - 1) + j, tm), cols] * cw[j:j + 1]
            tail = xbuf[tm:tm + SUBLANES, cols]
            xbuf[0:SUBLANES, cols] = tail
            tail_ref[0, :, cols] = tail
        else:
            raw3 = raw.reshape(nb, SUBLANES, CONV_COLS)
            y = raw3 * cw[GDN_CONV_TAPS - 1:GDN_CONV_TAPS]
            for back in range(1, GDN_CONV_TAPS):
                prev = pltpu.roll(raw3, back, axis=1)
                for t_row in range(back):
                    h_row = SUBLANES - back + t_row
                    prev = jnp.where(t_pos == t_row, hist_ref[:, h_row:h_row + 1, cols], prev)
                y = y + prev * cw[GDN_CONV_TAPS - 1 - back:GDN_CONV_TAPS - back]
            y = y.reshape(tm, CONV_COLS)
            tail_ref[:, :, cols] = raw3
        qkv_ref[:, cols] = _silu(y)
    z_ref[...] = jnp.dot(xb, w_z_ref[...], preferred_element_type=F32)
    rest = jnp.dot(xb, w_rest_ref[...], preferred_element_type=F32)
    cq = rest[:, 0:MLA_Q_RANK]
    ckv_raw = rest[:, MLA_Q_RANK:MLA_Q_RANK + MLA_KV_RANK]
    small = rest[:, MLA_Q_RANK + MLA_KV_RANK:]
    small_ref[...] = small

    c_new = _rms(ckv_raw, kvn_ref[...])
    ckv_ref[...] = c_new
    cqn = _rms(cq, qn_ref[...])
    n_nope = MLA_H * MLA_NOPE_D
    npe = MLA_H * MLA_ROPE_D
    k_rot_a = small[:, 16:16 + MLA_ROPE_D]
    k_rot_b = small[:, 16 + MLA_ROPE_D:16 + 2 * MLA_ROPE_D]

    if transposed_q:
        cos_t = cos_ref[...]
        sin_t = sin_ref[...]
        tm = x_ref.shape[0]
        kpe = k_rot_a * cos_k_ref[...] + k_rot_b * sin_k_ref[...]
        ct_ref[0:MLA_KV_RANK, :] = lax.dot_general(_eye(LANES, BF16), c_new.astype(BF16), _NT,
                                                   preferred_element_type=F32).astype(ct_ref.dtype)
        ct_ref[MLA_KV_RANK:, :] = jnp.ones((CT_ROWS - MLA_KV_RANK, tm), ct_ref.dtype)
        q_t = lax.dot_general(w_q_ref[...], cqn.astype(BF16), _NT,
                              preferred_element_type=F32)
        scale = MLA_SOFTMAX_SCALE * LOG2_E
        q_pe = (q_t[n_nope:n_nope + npe] * cos_t + q_t[n_nope + npe:] * sin_t) * scale
        for h in range(MLA_H):
            q_lat = _mm(w_uk_ref[h], q_t[h * MLA_NOPE_D:(h + 1) * MLA_NOPE_D]) * scale
            cols = slice(h * tm, (h + 1) * tm)
            q_ref[0:MLA_KV_RANK, cols] = q_lat.astype(q_ref.dtype)
            q_ref[MLA_KV_RANK:MLA_QK, cols] = q_pe[h * MLA_ROPE_D:(h + 1) * MLA_ROPE_D].astype(q_ref.dtype)
    else:
        cos = cos_ref[...]
        sin = sin_ref[...]
        kpe = k_rot_a * cos[:, 0:MLA_ROPE_D] + k_rot_b * sin[:, 0:MLA_ROPE_D]
        q = _mm(cqn, w_q_ref[...])
        q_pe = (q[:, n_nope:n_nope + npe] * cos + q[:, n_nope + npe:] * sin) * MLA_SOFTMAX_SCALE
        for h in range(MLA_H):
            q_lat = _mm(q[:, h * MLA_NOPE_D:(h + 1) * MLA_NOPE_D], w_uk_ref[h]) * MLA_SOFTMAX_SCALE
            q_ref[h, :, 0:MLA_KV_RANK] = q_lat.astype(q_ref.dtype)
            q_ref[h, :, MLA_KV_RANK:MLA_QK] = q_pe[:, h * MLA_ROPE_D:(h + 1) * MLA_ROPE_D].astype(q_ref.dtype)

    kpe_ref[...] = kpe
    k_ref[:, 0:MLA_KV_RANK] = c_new.astype(k_ref.dtype)
    k_ref[:, MLA_KV_RANK:MLA_QK] = kpe.astype(k_ref.dtype)


def _in_proj(x2, hist8, wts, rope, tm, t, transposed_q):
    m, d = x2.shape
    b = m // t
    row = lambda i: (i, 0)
    npe = MLA_H * MLA_ROPE_D
    act_dtype = BF16 if transposed_q else F32
    if transposed_q:
        tiles_per_seq = t // tm
        seq_spec = pl.BlockSpec((1, SUBLANES, GDN_QKV), lambda i: (i // tiles_per_seq, 0, 0))
        scratch = [pltpu.VMEM((tm + SUBLANES, GDN_QKV), F32)]
    else:
        tiles_per_seq = 1
        seq_spec = pl.BlockSpec((tm // SUBLANES, SUBLANES, GDN_QKV), lambda i: (i, 0, 0))
        scratch = []
    out_shape = [
        jax.ShapeDtypeStruct((m, GDN_QKV), F32),
        jax.ShapeDtypeStruct((b, SUBLANES, GDN_QKV), F32),
        jax.ShapeDtypeStruct((m, GDN_QK), F32),
        jax.ShapeDtypeStruct((m, LANES), F32),
        jax.ShapeDtypeStruct((m, MLA_KV_RANK), F32),
        jax.ShapeDtypeStruct((m, MLA_ROPE_D), F32),
    ]
    out_specs = [
        pl.BlockSpec((tm, GDN_QKV), row), seq_spec, pl.BlockSpec((tm, GDN_QK), row),
        pl.BlockSpec((tm, LANES), row), pl.BlockSpec((tm, MLA_KV_RANK), row),
        pl.BlockSpec((tm, MLA_ROPE_D), row),
    ]
    if transposed_q:
        n_rope_tiles = rope[0].shape[1] // tm
        rope_specs = [pl.BlockSpec((npe, tm), lambda i: (0, i % n_rope_tiles))] * 2
        rope_specs += [pl.BlockSpec((tm, MLA_ROPE_D), lambda i: (i % n_rope_tiles, 0))] * 2
        w_q, w_uk = wts["w_q_t"], wts["w_uk"]
        out_shape += [jax.ShapeDtypeStruct((m // tm, MLA_QK, MLA_H * tm), act_dtype),
                      jax.ShapeDtypeStruct((m, MLA_QK), act_dtype),
                      jax.ShapeDtypeStruct((CT_ROWS, m), act_dtype)]
        out_specs += [pl.BlockSpec((None, MLA_QK, MLA_H * tm), lambda i: (i, 0, 0)),
                      pl.BlockSpec((tm, MLA_QK), row),
                      pl.BlockSpec((CT_ROWS, tm), lambda i: (0, i))]
    else:
        n_rope_tiles = rope[0].shape[0] // tm
        rope_specs = [pl.BlockSpec((tm, npe), lambda i: (i % n_rope_tiles, 0))] * 2
        w_q, w_uk = wts["w_q"], wts["w_uk_t"]
        out_shape += [jax.ShapeDtypeStruct((MLA_H, m, MLA_QK), act_dtype),
                      jax.ShapeDtypeStruct((m, MLA_QK), act_dtype)]
        out_specs += [pl.BlockSpec((MLA_H, tm, MLA_QK), lambda i: (0, i, 0)),
                      pl.BlockSpec((tm, MLA_QK), row)]
    return pl.pallas_call(
        functools.partial(_in_proj_kernel, transposed_q=transposed_q, tiles_per_seq=tiles_per_seq),
        grid=(m // tm,),
        in_specs=[
            pl.BlockSpec((tm, d), row),
            _const_spec(wts["w_qkv"].shape), _const_spec(wts["w_z"].shape),
            _const_spec(wts["w_rest"].shape), _const_spec(w_q.shape), _const_spec(w_uk.shape),
            _const_spec(wts["q_norm"].shape), _const_spec(wts["kv_norm"].shape),
            _const_spec(wts["gdn_conv_w"].shape), seq_spec,
            *rope_specs,
        ],
        out_specs=tuple(out_specs),
        out_shape=tuple(out_shape),
        scratch_shapes=scratch,
        compiler_params=pltpu.CompilerParams(
            dimension_semantics=("arbitrary",), vmem_limit_bytes=VMEM_LIMIT_BYTES),
        name="in_proj",
    )(x2, wts["w_qkv"], wts["w_z"], wts["w_rest"], w_q, w_uk, wts["q_norm"], wts["kv_norm"],
      wts["gdn_conv_w"], hist8, *rope)


def _gdn_kernel(qkv_ref, z_ref, small_ref, s0_ref, gpar_ref, normw_ref,
                tril_ref, bones_ref, eb_ref, eg_ref,
                o_ref, sout_ref, s_scr, o_buf, *, t_in, tt, n_seq):
    t = pl.program_id(1)
    c_rows, gw = GDN_CHUNK, GDN_GROUP
    padded = tt > t_in
    rows_all = n_seq * tt

    @pl.when(t == 0)
    def _():
        s_scr[...] = jnp.zeros(s_scr.shape, F32)
        for s in range(n_seq):
            for h in range(GDN_H):
                g, hh = divmod(h, HEADS_PER_GROUP)
                s_scr[s * GDN_GROUPS + g, hh * GDN_D:(hh + 1) * GDN_D,
                      hh * GDN_D:(hh + 1) * GDN_D] = s0_ref[s, h]

    def gather_rows(ref, width):
        parts = []
        for s in range(n_seq):
            parts.append(ref[s])
            if padded:
                parts.append(jnp.zeros((tt - t_in, width), F32))
        return parts[0] if len(parts) == 1 else jnp.concatenate(parts, axis=0)

    qkv = gather_rows(qkv_ref, GDN_QKV)
    small = gather_rows(small_ref, LANES)
    z = gather_rows(z_ref, GDN_QK)
    if padded:
        row_valid = lax.broadcasted_iota(jnp.int32, (rows_all, 1), 0) % tt < t_in
    beta_all = _sigmoid(small)
    xa = small + gpar_ref[1:2, :]
    softplus = jnp.maximum(xa, 0.0) + jnp.log1p(jnp.exp(-jnp.abs(xa)))
    g_all = -jnp.exp(gpar_ref[0:1, :]) * softplus
    if padded:
        beta_all = jnp.where(row_valid, beta_all, 0.0)
        g_all = jnp.where(row_valid, g_all, 0.0)

    def split3(x):
        hi = x.astype(BF16)
        r1 = x - hi.astype(F32)
        mid = r1.astype(BF16)
        return hi, mid, (r1 - mid.astype(F32)).astype(BF16)

    def mm_exact(x, ones_mat, terms=3):
        return sum(jnp.dot(part, ones_mat, preferred_element_type=F32) for part in split3(x)[:terms])

    def mm_exact_left(ones_mat, x):
        return sum(jnp.dot(ones_mat, part, preferred_element_type=F32) for part in split3(x))

    gc_all = mm_exact_left(tril_ref[...], g_all)
    beta_x = mm_exact(beta_all, eb_ref[...])
    gc_x = mm_exact(gc_all, eg_ref[...])
    block_ones = bones_ref[...]

    def block_diag(x):
        return jnp.concatenate([x.astype(BF16)] * HEADS_PER_GROUP, axis=0) * block_ones

    rr = lax.broadcasted_iota(jnp.int32, (c_rows, gw), 0)
    cc = lax.broadcasted_iota(jnp.int32, (c_rows, gw), 1) % c_rows
    incl, strict, diag = rr >= cc, rr > cc, rr == cc
    eye_t = diag.astype(F32)
    n_levels = int(math.log2(c_rows)) - 1
    n_chunks = rows_all // c_rows
    chunks_per_seq = tt // c_rows

    q_n, k_n = [], []
    for g in range(GDN_GROUPS):
        q_g = qkv[:, g * gw:(g + 1) * gw]
        k_g = qkv[:, GDN_QK + g * gw:GDN_QK + (g + 1) * gw]
        q_n.append(q_g * lax.rsqrt(mm_exact(q_g * q_g, block_ones, 2) + L2_EPS) * (GDN_D ** -0.5))
        k_n.append(k_g * lax.rsqrt(mm_exact(k_g * k_g, block_ones, 2) + L2_EPS))

    units = [(g, c) for c in range(n_chunks) for g in range(GDN_GROUPS)]
    st = {}
    for g, c in units:
        rows = slice(c * c_rows, (c + 1) * c_rows)
        lanes = slice(g * gw, (g + 1) * gw)
        q_c, k_c = q_n[g][rows], k_n[g][rows]
        v_c = qkv[rows, 2 * GDN_QK + g * gw:2 * GDN_QK + (g + 1) * gw]
        b_c = beta_x[rows, lanes]
        gc_c = gc_x[rows, lanes]
        g_row = jnp.sum(jnp.where(diag, gc_c, 0.0), axis=0, keepdims=True)
        g_last = gc_c[c_rows - 1:c_rows, :]
        decay = jnp.exp(jnp.where(incl, gc_c - g_row, -jnp.inf))
        e_col = jnp.exp(gc_c)
        kb = k_c * b_c
        kk_qk = lax.dot_general(jnp.concatenate([kb, q_c], axis=0).astype(BF16), block_diag(k_c),
                                _NT, preferred_element_type=F32)
        lmat = jnp.where(strict, kk_qk[0:c_rows] * decay, 0.0)
        st[g, c] = dict(
            inv=eye_t - lmat, lmat=lmat,
            a=(kk_qk[c_rows:] * decay).astype(BF16),
            vb=block_diag(v_c * b_c), kbe=block_diag(kb * e_col),
            qg=q_c * e_col,
            kg=(k_c * jnp.exp(g_last - gc_c)).astype(BF16),
            s_decay=jnp.exp(g_last),
        )
    for u in units:
        st[u]["pw"] = _mm(st[u]["lmat"], block_diag(st[u]["lmat"]))
    for lvl in range(n_levels):
        for u in units:
            s_u = st[u]
            bd = block_diag(s_u["pw"])
            if lvl < n_levels - 1:
                both = _mm(jnp.concatenate([s_u["pw"], s_u["inv"]], axis=0), bd)
                s_u["pw"] = both[0:c_rows]
                s_u["inv"] = s_u["inv"] + both[c_rows:]
            else:
                s_u["inv"] = s_u["inv"] + _mm(s_u["inv"], bd)
    for u in units:
        inv_b = st[u]["inv"].astype(BF16)
        st[u]["u"] = jnp.dot(inv_b, st[u]["vb"], preferred_element_type=F32)
        st[u]["w"] = jnp.dot(inv_b, st[u]["kbe"], preferred_element_type=F32)

    n_states = n_seq * GDN_GROUPS
    s_bd = [s_scr[i] for i in range(n_states)]
    block_ones_f = block_ones.astype(F32)
    for g, c in units:
        p = st[g, c]
        si = (c // chunks_per_seq) * GDN_GROUPS + g
        both = _mm(jnp.concatenate([p["w"], p["qg"]], axis=0), s_bd[si])
        v_new = p["u"] - both[0:c_rows]
        o_buf[c * c_rows:(c + 1) * c_rows, g * gw:(g + 1) * gw] = (
            both[c_rows:] + jnp.dot(p["a"], block_diag(v_new), preferred_element_type=F32))
        upd = lax.dot_general(p["kg"], v_new.astype(BF16), _TN, preferred_element_type=F32)
        s_bd[si] = s_bd[si] * p["s_decay"] + upd * block_ones_f
    for i in range(n_states):
        s_scr[i] = s_bd[i]

    norm_w = jnp.concatenate([normw_ref[...]] * HEADS_PER_GROUP, axis=1)
    outs = []
    for g in range(GDN_GROUPS):
        o_g = o_buf[:, g * gw:(g + 1) * gw]
        ms = mm_exact(o_g * o_g, block_ones, 2) * (1.0 / GDN_D)
        outs.append(o_g * lax.rsqrt(ms + RMS_EPS) * norm_w * _silu(z[:, g * gw:(g + 1) * gw]))
    o_all = jnp.concatenate(outs, axis=1)
    for s in range(n_seq):
        o_ref[s] = o_all[s * tt:s * tt + t_in].astype(o_ref.dtype)

    @pl.when(t == pl.num_programs(1) - 1)
    def _():
        for s in range(n_seq):
            for h in range(GDN_H):
                g, hh = divmod(h, HEADS_PER_GROUP)
                sout_ref[s, h] = s_scr[s * GDN_GROUPS + g, hh * GDN_D:(hh + 1) * GDN_D,
                                       hh * GDN_D:(hh + 1) * GDN_D]


def _gdn_constants(tt):
    tril = np.kron(np.eye(tt // GDN_CHUNK), np.tril(np.ones((GDN_CHUNK, GDN_CHUNK))))
    block_ones = np.kron(np.eye(HEADS_PER_GROUP), np.ones((GDN_D, GDN_D)))
    spread = np.kron(np.eye(GDN_H), np.ones((1, GDN_D)))
    e_b = np.zeros((LANES, GDN_QK))
    e_g = np.zeros((LANES, GDN_QK))
    e_b[0:GDN_H] = spread
    e_g[GDN_H:2 * GDN_H] = spread
    return [jnp.asarray(a, BF16) for a in (tril, block_ones, e_b, e_g)]


def _gdn(qkv, z, small, s0, gpar, norm_w, t_in, n_seq, act_dtype):
    b, t, _ = qkv.shape
    tt = max(t_in, GDN_CHUNK)
    consts = _gdn_constants(n_seq * tt)
    seq = lambda i, j: (i, j, 0)
    per_b4 = lambda i, j: (i, 0, 0, 0)
    return pl.pallas_call(
        functools.partial(_gdn_kernel, t_in=t_in, tt=tt, n_seq=n_seq),
        grid=(b // n_seq, t // t_in),
        in_specs=[
            pl.BlockSpec((n_seq, t_in, GDN_QKV), seq), pl.BlockSpec((n_seq, t_in, GDN_QK), seq),
            pl.BlockSpec((n_seq, t_in, LANES), seq),
            pl.BlockSpec((n_seq, GDN_H, GDN_D, GDN_D), per_b4),
            _const_spec(gpar.shape), _const_spec(norm_w.shape),
            *[_const_spec(c.shape) for c in consts],
        ],
        out_specs=(pl.BlockSpec((n_seq, t_in, GDN_QK), seq),
                   pl.BlockSpec((n_seq, GDN_H, GDN_D, GDN_D), per_b4)),
        out_shape=(jax.ShapeDtypeStruct((b, t, GDN_QK), act_dtype),
                   jax.ShapeDtypeStruct((b, GDN_H, GDN_D, GDN_D), F32)),
        scratch_shapes=[pltpu.VMEM((n_seq * GDN_GROUPS, GDN_GROUP, GDN_GROUP), F32),
                        pltpu.VMEM((n_seq * tt, GDN_QK), F32)],
        compiler_params=pltpu.CompilerParams(
            dimension_semantics=("parallel", "arbitrary"), vmem_limit_bytes=VMEM_LIMIT_BYTES),
        name="gdn",
    )(qkv, z, small, s0, gpar, norm_w, *consts)


def _attn_prompt_kernel(qi_ref, kj_ref, qt_ref, k_ref, ct_ref, w_uvt_ref, o_ref, m_scr, acc_scr,
                        *, tq, tk):
    p = pl.program_id(1)
    i = qi_ref[p]
    j = kj_ref[p]
    kv_per_q = tq // tk
    first_diag = i * kv_per_q

    @pl.when(j == 0)
    def _():
        m_scr[...] = jnp.full(m_scr.shape, -jnp.inf, F32)
        acc_scr[...] = jnp.zeros(acc_scr.shape, F32)

    def update(masked):
        s = jnp.dot(k_ref[...], qt_ref[...], preferred_element_type=F32)
        if masked:
            key = lax.broadcasted_iota(jnp.int32, s.shape, 0) + (j - first_diag) * tk
            qry = lax.broadcasted_iota(jnp.int32, s.shape, 1) % tq
            s = jnp.where(key <= qry, s, -jnp.inf)
        m_prev = m_scr[...]
        m_new = jnp.maximum(m_prev, jnp.max(s, axis=0, keepdims=True))
        alpha = jnp.exp2(m_prev - m_new)
        pmat = jnp.exp2(s - m_new)
        acc_scr[...] = alpha * acc_scr[...] + jnp.dot(ct_ref[...], pmat.astype(BF16),
                                                      preferred_element_type=F32)
        m_scr[...] = m_new

    @pl.when(j < first_diag)
    def _():
        update(False)

    @pl.when(j >= first_diag)
    def _():
        update(True)

    @pl.when(j == first_diag + kv_per_q - 1)
    def _():
        o_lat_t = acc_scr[0:MLA_KV_RANK, :] / acc_scr[MLA_KV_RANK:MLA_KV_RANK + 1, :]
        outs = [_mm(w_uvt_ref[h], o_lat_t[:, h * tq:(h + 1) * tq]) for h in range(MLA_H)]
        o_ref[...] = jnp.concatenate(outs, axis=0).T.astype(o_ref.dtype)


def _attn_prompt(q_t, k, c_t, w_uv_t, b, t, tq, tk):
    nq, nk = t // tq, t // tk
    kv_per_q = tq // tk
    m = b * t
    assert q_t.shape == (m // tq, MLA_QK, MLA_H * tq), "query tiles must match the projection tiles"
    pairs = [(i, j) for i in range(nq) for j in range((i + 1) * kv_per_q)]
    qi = jnp.asarray(np.array([p[0] for p in pairs], np.int32))
    kj = jnp.asarray(np.array([p[1] for p in pairs], np.int32))
    grid_spec = pltpu.PrefetchScalarGridSpec(
        num_scalar_prefetch=2,
        grid=(b, len(pairs)),
        in_specs=[
            pl.BlockSpec((None, MLA_QK, MLA_H * tq), lambda bi, p, qi, kj: (bi * nq + qi[p], 0, 0)),
            pl.BlockSpec((tk, MLA_QK), lambda bi, p, qi, kj: (bi * nk + kj[p], 0)),
            pl.BlockSpec((CT_ROWS, tk), lambda bi, p, qi, kj: (0, bi * nk + kj[p])),
            pl.BlockSpec(w_uv_t.shape, lambda bi, p, qi, kj: (0, 0, 0)),
        ],
        out_specs=pl.BlockSpec((tq, MLA_H * MLA_V_D), lambda bi, p, qi, kj: (bi * nq + qi[p], 0)),
        scratch_shapes=[pltpu.VMEM((1, MLA_H * tq), F32),
                        pltpu.VMEM((CT_ROWS, MLA_H * tq), F32)],
    )
    return pl.pallas_call(
        functools.partial(_attn_prompt_kernel, tq=tq, tk=tk),
        grid_spec=grid_spec,
        out_shape=jax.ShapeDtypeStruct((m, MLA_H * MLA_V_D), BF16),
        compiler_params=pltpu.CompilerParams(
            dimension_semantics=("parallel", "arbitrary"), vmem_limit_bytes=VMEM_LIMIT_BYTES),
        name="attn_prompt",
    )(qi, kj, q_t, k, c_t, w_uv_t)


def _softmax_update(s, v_b, m_scr, l_scr, acc_scr):
    m_prev = m_scr[...]
    m_new = jnp.maximum(m_prev, jnp.max(s, axis=-1, keepdims=True))
    alpha = jnp.exp(m_prev - m_new)
    p = jnp.exp(s - m_new)
    l_scr[...] = alpha * l_scr[...] + jnp.sum(p, axis=-1, keepdims=True)
    acc_scr[...] = alpha * acc_scr[...] + jnp.dot(p.astype(BF16), v_b, preferred_element_type=F32)
    m_scr[...] = m_new


def _attn_sample_kernel(pt_ref, q_ref, knew_ref, w_uv_ref, ckv_hbm, kpe_hbm, o_ref,
                        c_buf, kpe_buf, sems, m_scr, l_scr, acc_scr, *,
                        layer, pages_per_step, page, t_new):
    seq = pl.program_id(0)
    grp = pl.program_id(1)
    n_grp = pl.num_programs(1)
    step = seq * n_grp + grp
    slot = step % 2
    rows = MLA_H * t_new

    def page_copies(seq_i, grp_i, slot_i, p):
        pid = pt_ref[seq_i, grp_i * pages_per_step + p]
        return (
            pltpu.make_async_copy(ckv_hbm.at[layer, pid],
                                  c_buf.at[slot_i, pl.ds(p * page, page), :], sems.at[0, slot_i]),
            pltpu.make_async_copy(kpe_hbm.at[layer, pid],
                                  kpe_buf.at[slot_i, :, pl.ds(p * page, page)], sems.at[1, slot_i]),
        )

    def start_fetch(seq_i, grp_i, slot_i):
        for p in range(pages_per_step):
            for cp in page_copies(seq_i, grp_i, slot_i, p):
                cp.start()

    @pl.when(step == 0)
    def _():
        start_fetch(seq, grp, slot)

    @pl.when(step + 1 < pl.num_programs(0) * n_grp)
    def _():
        nxt = step + 1
        start_fetch(nxt // n_grp, nxt % n_grp, 1 - slot)

    @pl.when(grp == 0)
    def _():
        m_scr[...] = jnp.full(m_scr.shape, -jnp.inf, F32)
        l_scr[...] = jnp.zeros(l_scr.shape, F32)
        acc_scr[...] = jnp.zeros(acc_scr.shape, F32)

    for p in range(pages_per_step):
        for cp in page_copies(seq, grp, slot, p):
            cp.wait()

    q = q_ref[...].reshape(rows, MLA_QK).astype(BF16)
    c_b = c_buf[slot].astype(BF16)
    s = (lax.dot_general(q[:, 0:MLA_KV_RANK], c_b, _NT, preferred_element_type=F32)
         + jnp.dot(q[:, MLA_KV_RANK:MLA_QK], kpe_buf[slot].astype(BF16), preferred_element_type=F32))
    _softmax_update(s, c_b, m_scr, l_scr, acc_scr)

    @pl.when(grp == n_grp - 1)
    def _():
        k_new = knew_ref[...]
        s = _mm_nt(q, k_new)
        qi = lax.broadcasted_iota(jnp.int32, (rows, t_new), 0) % t_new
        ki = lax.broadcasted_iota(jnp.int32, (rows, t_new), 1)
        s = jnp.where(ki <= qi, s, -jnp.inf)
        _softmax_update(s, k_new[:, 0:MLA_KV_RANK].astype(BF16), m_scr, l_scr, acc_scr)
        o_lat = acc_scr[...] / l_scr[...]
        outs = [_mm(o_lat[h * t_new:(h + 1) * t_new], w_uv_ref[h]) for h in range(MLA_H)]
        o_ref[...] = jnp.concatenate(outs, axis=-1).astype(o_ref.dtype)


def _attn_sample(q, k_new, cache_ckv, cache_kpe, layer, page_table, w_uv, b, t_new):
    cache_kpe = jnp.swapaxes(cache_kpe, 2, 3)
    n_pages = page_table.shape[1]
    page = cache_ckv.shape[2]
    pages_per_step = math.gcd(SAMPLE_PAGES_PER_STEP, n_pages)
    rows = MLA_H * t_new
    n_steps = n_pages // pages_per_step

    keys = pages_per_step * page
    grid_spec = pltpu.PrefetchScalarGridSpec(
        num_scalar_prefetch=1,
        grid=(b, n_steps),
        in_specs=[
            pl.BlockSpec((MLA_H, t_new, MLA_QK), lambda bi, s, pt: (0, bi, 0)),
            pl.BlockSpec((t_new, MLA_QK), lambda bi, s, pt: (bi, 0)),
            pl.BlockSpec(w_uv.shape, lambda bi, s, pt: (0, 0, 0)),
            pl.BlockSpec(memory_space=pl.ANY),
            pl.BlockSpec(memory_space=pl.ANY),
        ],
        out_specs=pl.BlockSpec((t_new, MLA_H * MLA_V_D), lambda bi, s, pt: (bi, 0)),
        scratch_shapes=[pltpu.VMEM((2, keys, MLA_KV_RANK), F32),
                        pltpu.VMEM((2, MLA_ROPE_D, keys), F32),
                        pltpu.SemaphoreType.DMA((2, 2)),
                        pltpu.VMEM((rows, 1), F32), pltpu.VMEM((rows, 1), F32),
                        pltpu.VMEM((rows, MLA_KV_RANK), F32)],
    )
    return pl.pallas_call(
        functools.partial(_attn_sample_kernel, layer=layer, pages_per_step=pages_per_step,
                          page=page, t_new=t_new),
        grid_spec=grid_spec,
        out_shape=jax.ShapeDtypeStruct((b * t_new, MLA_H * MLA_V_D), F32),
        compiler_params=pltpu.CompilerParams(
            dimension_semantics=("arbitrary", "arbitrary"), vmem_limit_bytes=VMEM_LIMIT_BYTES),
        name="attn_sample",
    )(page_table, q, k_new, w_uv, cache_ckv, cache_kpe)


def _ffn_kernel(x_ref, og_ref, om_ref, hist_ref, w_out_ref, ln_ref, w_up_ref, cw_ref, w_dn_ref,
                y_ref, nf_ref, *scratch, tm, alpha, short_seq):
    d_gdn = og_ref.shape[-1]
    n_chunks, _, two_fc = w_up_ref.shape
    fc = two_fc // 2
    mix = _mm(og_ref[...], w_out_ref[0:d_gdn, :]) + _mm(om_ref[...], w_out_ref[d_gdn:, :])
    x1 = _layernorm(alpha * x_ref[...] + mix, ln_ref[0:1, :], ln_ref[1:2, :])
    x1_b = x1.astype(BF16)

    if short_seq:
        h_buf, = scratch
        nb = tm // SUBLANES
        t_pos = lax.broadcasted_iota(jnp.int32, (nb, SUBLANES, two_fc), 1)
    else:
        h_buf, ubuf, carry = scratch

        @pl.when(pl.program_id(1) == 0)
        def _():
            carry[...] = hist_ref[0]

    for c in range(n_chunks):
        cols = slice(c * two_fc, (c + 1) * two_fc)
        up = jnp.dot(x1_b, w_up_ref[c], preferred_element_type=F32)
        cw = cw_ref[c]
        if short_seq:
            up3 = up.reshape(nb, SUBLANES, two_fc)
            h0 = hist_ref[:, SUBLANES - 2:SUBLANES - 1, cols]
            h1 = hist_ref[:, SUBLANES - 1:SUBLANES, cols]
            prev1 = jnp.where(t_pos == 0, h1, pltpu.roll(up3, 1, axis=1))
            prev2 = jnp.where(t_pos == 0, h0, jnp.where(t_pos == 1, h1, pltpu.roll(up3, 2, axis=1)))
            hc = (prev2 * cw[0:1] + prev1 * cw[1:2] + up3 * cw[2:3]).reshape(tm, two_fc)
            nf_ref[:, :, cols] = up3
        else:
            ubuf[0:SUBLANES, :] = carry[:, cols]
            ubuf[SUBLANES:SUBLANES + tm, :] = up
            hc = (ubuf[pl.ds(SUBLANES - 2, tm), :] * cw[0:1]
                  + ubuf[pl.ds(SUBLANES - 1, tm), :] * cw[1:2] + up * cw[2:3])
            last = ubuf[tm:tm + SUBLANES, :]
            carry[:, cols] = last
            nf_ref[0, :, cols] = last
        h_buf[:, c * fc:(c + 1) * fc] = (_silu(hc[:, fc:]) * hc[:, 0:fc]).astype(BF16)
    down = jnp.dot(h_buf[...], w_dn_ref[...], preferred_element_type=F32)
    y_ref[...] = _layernorm(alpha * x1 + down, ln_ref[2:3, :], ln_ref[3:4, :])


def _ffn(x2, o_gdn, o_mla, hist8, wts, b, t, tm, alpha):
    m, d = x2.shape
    n_chunks, _, two_fc = wts["w_up"].shape
    two_ff = n_chunks * two_fc
    short_seq = t == SUBLANES
    if short_seq:
        grid = (m // tm, 1)
        nb = tm // SUBLANES
        row = lambda i, j: (i, 0)
        hist_spec = pl.BlockSpec((nb, SUBLANES, two_ff), lambda i, j: (i, 0, 0))
        scratch = [pltpu.VMEM((tm, two_ff // 2), BF16)]
    else:
        nt = t // tm
        grid = (b, nt)
        row = lambda i, j: (i * nt + j, 0)
        hist_spec = pl.BlockSpec((1, SUBLANES, two_ff), lambda i, j: (i, 0, 0))
        scratch = [pltpu.VMEM((tm, two_ff // 2), BF16),
                   pltpu.VMEM((tm + SUBLANES, two_fc), F32), pltpu.VMEM((SUBLANES, two_ff), F32)]
    return pl.pallas_call(
        functools.partial(_ffn_kernel, tm=tm, alpha=alpha, short_seq=short_seq),
        grid=grid,
        in_specs=[
            pl.BlockSpec((tm, d), row), pl.BlockSpec((tm, o_gdn.shape[1]), row),
            pl.BlockSpec((tm, o_mla.shape[1]), row), hist_spec,
            _const_spec(wts["w_out"].shape), _const_spec(wts["ln"].shape),
            _const_spec(wts["w_up"].shape), _const_spec(wts["ffn_cw"].shape),
            _const_spec(wts["w_dn"].shape),
        ],
        out_specs=(pl.BlockSpec((tm, d), row), hist_spec),
        out_shape=(jax.ShapeDtypeStruct((m, d), F32),
                   jax.ShapeDtypeStruct((b, SUBLANES, two_ff), F32)),
        scratch_shapes=scratch,
        compiler_params=pltpu.CompilerParams(
            dimension_semantics=("parallel", "arbitrary"), vmem_limit_bytes=VMEM_LIMIT_BYTES),
        name="ffn",
    )(x2, o_gdn, o_mla, hist8, wts["w_out"], wts["ln"], wts["w_up"], wts["ffn_cw"], wts["w_dn"])


def _swap_halves(w):
    half = w.shape[-1] // 2
    return jnp.concatenate([-w[..., half:], w[..., :half]], axis=-1)


def _prep_weights(w_in, gdn_conv_w, gdn_A_log, gdn_dt_bias, gdn_norm_w, mla_q_norm_w, mla_w_uq,
                  mla_kv_norm_w, mla_w_uk, mla_w_uv, w_out, ln1_g, ln1_b, ffn_w_up, ffn_conv_w,
                  ffn_w_down, ln2_g, ln2_b):
    d_model = w_in.shape[0]
    o = 0
    w_qkv = w_in[:, o:o + GDN_QKV]; o += GDN_QKV
    w_z = w_in[:, o:o + GDN_QK]; o += GDN_QK
    w_b = w_in[:, o:o + GDN_H]; o += GDN_H
    w_a = w_in[:, o:o + GDN_H]; o += GDN_H
    w_cq = w_in[:, o:o + MLA_Q_RANK]; o += MLA_Q_RANK
    w_ckv = w_in[:, o:o + MLA_KV_RANK]; o += MLA_KV_RANK
    w_kpe = w_in[:, o:o + MLA_ROPE_D]
    pad = jnp.zeros((d_model, LANES - 2 * GDN_H - 2 * MLA_ROPE_D), w_in.dtype)
    w_small = jnp.concatenate([w_b, w_a, w_kpe, _swap_halves(w_kpe), pad], axis=1)
    w_rest = jnp.concatenate([w_cq, w_ckv, w_small], axis=1)

    uq = mla_w_uq.reshape(MLA_Q_RANK, MLA_H, MLA_NOPE_D + MLA_ROPE_D)
    uq_nope = uq[:, :, :MLA_NOPE_D].reshape(MLA_Q_RANK, MLA_H * MLA_NOPE_D)
    uq_pe = uq[:, :, MLA_NOPE_D:]
    w_q = jnp.concatenate([uq_nope, uq_pe.reshape(MLA_Q_RANK, -1),
                           _swap_halves(uq_pe).reshape(MLA_Q_RANK, -1)], axis=1)

    gpar = jnp.zeros((2, LANES), F32)
    gpar = gpar.at[0, GDN_H:2 * GDN_H].set(gdn_A_log).at[1, GDN_H:2 * GDN_H].set(gdn_dt_bias)

    d_ff = ffn_w_down.shape[0]
    n_chunks = d_ff // FFN_CHUNK

    def interleave(w):
        lead = w.shape[:-1]
        w = w.reshape(lead + (2, n_chunks, FFN_CHUNK))
        return jnp.moveaxis(w, -3, -2).reshape(lead + (n_chunks, 2 * FFN_CHUNK))

    return {
        "w_qkv": w_qkv.astype(BF16), "w_z": w_z.astype(BF16), "w_rest": w_rest.astype(BF16),
        "w_q": w_q.astype(BF16), "w_q_t": w_q.T.astype(BF16),
        "w_uk_t": jnp.transpose(mla_w_uk, (1, 2, 0)).astype(BF16),
        "w_uk": jnp.transpose(mla_w_uk, (1, 0, 2)).astype(BF16),
        "w_uv": jnp.transpose(mla_w_uv, (1, 0, 2)).astype(BF16),
        "w_uv_t": jnp.transpose(mla_w_uv, (1, 2, 0)).astype(BF16),
        "q_norm": mla_q_norm_w.reshape(1, -1), "kv_norm": mla_kv_norm_w.reshape(1, -1),
        "gdn_conv_w": gdn_conv_w, "gpar": gpar, "gdn_norm_w": gdn_norm_w.reshape(1, -1),
        "w_out": w_out.astype(BF16),
        "ln": jnp.stack([ln1_g, ln1_b, ln2_g, ln2_b]),
        "w_up": jnp.transpose(interleave(ffn_w_up), (1, 0, 2)).astype(BF16),
        "ffn_cw": jnp.transpose(interleave(ffn_conv_w), (1, 0, 2)),
        "w_dn": ffn_w_down.astype(BF16),
        "interleave": interleave, "n_chunks": n_chunks,
    }


def _rope_tables(past_len, t, tm, transposed):
    half = MLA_ROPE_D // 2
    pos = (past_len + jnp.arange(t, dtype=jnp.int32)).astype(F32)
    inv = ROPE_BASE ** (-jnp.arange(half, dtype=F32) / half)
    ang = pos[:, None] * inv
    reps = (max(tm // t, 1), 2 * MLA_H)
    cos, sin = jnp.tile(jnp.cos(ang), reps), jnp.tile(jnp.sin(ang), reps)
    if transposed:
        return cos.T, sin.T, cos[:, 0:MLA_ROPE_D], sin[:, 0:MLA_ROPE_D]
    return cos, sin


def _pad_hist(h):
    return jnp.pad(h, ((0, 0), (SUBLANES - h.shape[1], 0), (0, 0)))


def _tiles(b, t):
    if t == SUBLANES:
        rows = min(256, b * t)
        return dict(proj=rows, ffn=rows, gdn=t, gdn_seqs=math.gcd(4, b), attn_q=None, attn_k=None)
    return dict(proj=min(512, t), ffn=min(512, t), gdn=min(512, t), gdn_seqs=1,
                attn_q=min(512, t), attn_k=min(256, t))


def _layer(x, past, layer, s0, conv_hist, ffn_hist, wts, alpha):
    b, t, d = x.shape
    m = b * t
    x2 = x.reshape(m, d)
    tiles = _tiles(b, t)
    short_seq = past is not None
    past_len = past[2].shape[1] * past[0].shape[2] if short_seq else 0
    act_dtype = F32 if short_seq else BF16
    rope = _rope_tables(past_len, t, tiles["proj"], transposed=not short_seq)
    qkv, conv_tail, z, small, c_new, kpe_new, q, k, *c_t = _in_proj(
        x2, _pad_hist(conv_hist), wts, rope, tiles["proj"], t, transposed_q=not short_seq)

    o_gdn, s_new = _gdn(qkv.reshape(b, t, -1), z.reshape(b, t, -1), small.reshape(b, t, -1),
                        s0, wts["gpar"], wts["gdn_norm_w"], tiles["gdn"], tiles["gdn_seqs"], act_dtype)
    if short_seq:
        cache_ckv, cache_kpe, page_table = past
        o_mla = _attn_sample(q, k, cache_ckv, cache_kpe, layer, page_table, wts["w_uv"], b, t)
    else:
        o_mla = _attn_prompt(q, k, c_t[0], wts["w_uv_t"], b, t, tiles["attn_q"], tiles["attn_k"])

    hist8 = wts["interleave"](_pad_hist(ffn_hist)).reshape(b, SUBLANES, -1)
    y, nf8 = _ffn(x2, o_gdn.reshape(m, -1), o_mla, hist8, wts, b, t, tiles["ffn"], alpha)

    n_chunks = wts["n_chunks"]
    nf = nf8[:, SUBLANES - (FFN_CONV_TAPS - 1):, :].reshape(b, FFN_CONV_TAPS - 1, n_chunks, 2, FFN_CHUNK)
    new_ffn = jnp.moveaxis(nf, -2, -3).reshape(b, FFN_CONV_TAPS - 1, -1)
    new_conv = conv_tail[:, SUBLANES - (GDN_CONV_TAPS - 1):, :]
    return (y.reshape(b, t, d), c_new.reshape(b, t, -1), kpe_new.reshape(b, t, -1), s_new,
            new_conv, new_ffn)


def kernel(x_prompt, x_sample, cache_ckv, cache_kpe, page_table, state_gdn, state_gdn_conv,
           state_ffn_conv, w_in, gdn_conv_w, gdn_A_log, gdn_dt_bias, gdn_norm_w, mla_q_norm_w,
           mla_w_uq, mla_kv_norm_w, mla_w_uk, mla_w_uv, w_out, ln1_g, ln1_b, ffn_w_up, ffn_conv_w,
           ffn_w_down, ln2_g, ln2_b):
    depth = w_in.shape[0]
    alpha = (2.0 * depth) ** 0.25
    bp, tp, _ = x_prompt.shape
    bs, ts, _ = x_sample.shape
    assert ts == SUBLANES, "sample group: one 8-row tile per sequence"
    xp, xs = x_prompt, x_sample
    new_p, new_s = [], []
    for l in range(depth):
        wts = _prep_weights(w_in[l], gdn_conv_w[l], gdn_A_log[l], gdn_dt_bias[l], gdn_norm_w[l],
                            mla_q_norm_w[l], mla_w_uq[l], mla_kv_norm_w[l], mla_w_uk[l],
                            mla_w_uv[l], w_out[l], ln1_g[l], ln1_b[l], ffn_w_up[l], ffn_conv_w[l],
                            ffn_w_down[l], ln2_g[l], ln2_b[l])
        two_ff = ffn_w_up.shape[-1]
        xp, *st_p = _layer(
            xp, None, l, jnp.zeros((bp, GDN_H, GDN_D, GDN_D), F32),
            jnp.zeros((bp, GDN_CONV_TAPS - 1, GDN_QKV), F32),
            jnp.zeros((bp, FFN_CONV_TAPS - 1, two_ff), F32), wts, alpha)
        xs, *st_s = _layer(
            xs, (cache_ckv, cache_kpe, page_table), l, state_gdn[l], state_gdn_conv[l],
            state_ffn_conv[l], wts, alpha)
        new_p.append(st_p)
        new_s.append(st_s)
    stack = lambda sts: [jnp.stack(v) for v in zip(*sts)]
    return (xp, xs, *stack(new_p), *stack(new_s))
```

```python
import functools
import math

import jax
import jax.numpy as jnp
import numpy as np
from jax import lax
from jax.experimental import pallas as pl
from jax.experimental.pallas import tpu as pltpu

F32 = jnp.float32
BF16 = jnp.bfloat16

GDN_H = 8
GDN_D = 64
GDN_QK = GDN_H * GDN_D
GDN_QKV = 3 * GDN_QK
GDN_CONV_TAPS = 4
GDN_CHUNK = 64
GDN_GROUP = 256
GDN_GROUPS = GDN_QK // GDN_GROUP
HEADS_PER_GROUP = GDN_GROUP // GDN_D
MLA_H = 8
MLA_Q_RANK = 256
MLA_KV_RANK = 128
MLA_NOPE_D = 64
MLA_ROPE_D = 32
MLA_V_D = 64
MLA_QK = MLA_KV_RANK + MLA_ROPE_D
MLA_SOFTMAX_SCALE = (MLA_NOPE_D + MLA_ROPE_D) ** -0.5
LOG2_E = math.log2(math.e)
CT_ROWS = MLA_KV_RANK + 16
ROPE_BASE = 10000.0
FFN_CONV_TAPS = 3
RMS_EPS = 1e-6
LN_EPS = 1e-5
L2_EPS = 1e-6

SUBLANES = 8
LANES = 128
VMEM_LIMIT_BYTES = 56 * 1024 * 1024
FFN_CHUNK = 256
CONV_COLS = 256
SAMPLE_PAGES_PER_STEP = 128

_NT = (((1,), (1,)), ((), ()))
_TN = (((0,), (0,)), ((), ()))


def _mm(a, b):
    return jnp.dot(a.astype(BF16), b.astype(BF16), preferred_element_type=F32)


def _mm_nt(a, b):
    return lax.dot_general(a.astype(BF16), b.astype(BF16), _NT, preferred_element_type=F32)


def _sigmoid(x):
    return 1.0 / (1.0 + jnp.exp(-x))


def _silu(x):
    return x * _sigmoid(x)


def _rms(x, w):
    return x * lax.rsqrt(jnp.mean(x * x, axis=-1, keepdims=True) + RMS_EPS) * w


def _layernorm(x, g, b):
    mu = jnp.mean(x, axis=-1, keepdims=True)
    xc = x - mu
    var = jnp.mean(xc * xc, axis=-1, keepdims=True)
    return xc * lax.rsqrt(var + LN_EPS) * g + b


def _eye(n, dtype):
    return (lax.broadcasted_iota(jnp.int32, (n, n), 0)
            == lax.broadcasted_iota(jnp.int32, (n, n), 1)).astype(dtype)


def _const_spec(shape):
    nd = len(shape)
    return pl.BlockSpec(shape, lambda *_: (0,) * nd, pipeline_mode=pl.Buffered(1))


def _in_proj_kernel(x_ref, w_qkv_ref, w_z_ref, w_rest_ref, w_q_ref, w_uk_ref, qn_ref, kvn_ref,
                    convw_ref, hist_ref, *refs, transposed_q, tiles_per_seq):
    if transposed_q:
        (cos_ref, sin_ref, cos_k_ref, sin_k_ref,
         qkv_ref, tail_ref, z_ref, small_ref, ckv_ref, kpe_ref, q_ref, k_ref, ct_ref, xbuf) = refs
    else:
        cos_ref, sin_ref, qkv_ref, tail_ref, z_ref, small_ref, ckv_ref, kpe_ref, q_ref, k_ref = refs
    tm = x_ref.shape[0]
    xb = x_ref[...].astype(BF16)
    if transposed_q:
        @pl.when(pl.program_id(0) % tiles_per_seq == 0)
        def _():
            xbuf[0:SUBLANES, :] = hist_ref[0]
    else:
        nb = tm // SUBLANES
        t_pos = lax.broadcasted_iota(jnp.int32, (nb, SUBLANES, CONV_COLS), 1)
    for c0 in range(0, GDN_QKV, CONV_COLS):
        cols = slice(c0, c0 + CONV_COLS)
        raw = jnp.dot(xb, w_qkv_ref[:, cols], preferred_element_type=F32)
        cw = convw_ref[:, cols]
        if transposed_q:
            xbuf[SUBLANES:SUBLANES + tm, cols] = raw
            y = raw * cw[GDN_CONV_TAPS - 1:GDN_CONV_TAPS]
            for j in range(GDN_CONV_TAPS - 1):
                y = y + xbuf[pl.ds(SUBLANES - (GDN_CONV_TAPS - 1) + j, tm), cols] * cw[j:j + 1]
            tail = xbuf[tm:tm + SUBLANES, cols]
            xbuf[0:SUBLANES, cols] = tail
            tail_ref[0, :, cols] = tail
        else:
            raw3 = raw.reshape(nb, SUBLANES, CONV_COLS)
            y = raw3 * cw[GDN_CONV_TAPS - 1:GDN_CONV_TAPS]
            for back in range(1, GDN_CONV_TAPS):
                prev = pltpu.roll(raw3, back, axis=1)
                for t_row in range(back):
                    h_row = SUBLANES - back + t_row
                    prev = jnp.where(t_pos == t_row, hist_ref[:, h_row:h_row + 1, cols], prev)
                y = y + prev * cw[GDN_CONV_TAPS - 1 - back:GDN_CONV_TAPS - back]
            y = y.reshape(tm, CONV_COLS)
            tail_ref[:, :, cols] = raw3
        qkv_ref[:, cols] = _silu(y)
    z_ref[...] = jnp.dot(xb, w_z_ref[...], preferred_element_type=F32)
    rest = jnp.dot(xb, w_rest_ref[...], preferred_element_type=F32)
    cq = rest[:, 0:MLA_Q_RANK]
    ckv_raw = rest[:, MLA_Q_RANK:MLA_Q_RANK + MLA_KV_RANK]
    small = rest[:, MLA_Q_RANK + MLA_KV_RANK:]
    small_ref[...] = small

    c_new = _rms(ckv_raw, kvn_ref[...])
    ckv_ref[...] = c_new
    cqn = _rms(cq, qn_ref[...])
    n_nope = MLA_H * MLA_NOPE_D
    npe = MLA_H * MLA_ROPE_D
    k_rot_a = small[:, 16:16 + MLA_ROPE_D]
    k_rot_b = small[:, 16 + MLA_ROPE_D:16 + 2 * MLA_ROPE_D]

    if transposed_q:
        cos_t = cos_ref[...]
        sin_t = sin_ref[...]
        tm = x_ref.shape[0]
        kpe = k_rot_a * cos_k_ref[...] + k_rot_b * sin_k_ref[...]
        ct_ref[0:MLA_KV_RANK, :] = lax.dot_general(_eye(LANES, BF16), c_new.astype(BF16), _NT,
                                                   preferred_element_type=F32).astype(ct_ref.dtype)
        ct_ref[MLA_KV_RANK:, :] = jnp.ones((CT_ROWS - MLA_KV_RANK, tm), ct_ref.dtype)
        q_t = lax.dot_general(w_q_ref[...], cqn.astype(BF16), _NT,
                              preferred_element_type=F32)
        scale = MLA_SOFTMAX_SCALE * LOG2_E
        q_pe = (q_t[n_nope:n_nope + npe] * cos_t + q_t[n_nope + npe:] * sin_t) * scale
        for h in range(MLA_H):
            q_lat = _mm(w_uk_ref[h], q_t[h * MLA_NOPE_D:(h + 1) * MLA_NOPE_D]) * scale
            cols = slice(h * tm, (h + 1) * tm)
            q_ref[0:MLA_KV_RANK, cols] = q_lat.astype(q_ref.dtype)
            q_ref[MLA_KV_RANK:MLA_QK, cols] = q_pe[h * MLA_ROPE_D:(h + 1) * MLA_ROPE_D].astype(q_ref.dtype)
    else:
        cos = cos_ref[...]
        sin = sin_ref[...]
        kpe = k_rot_a * cos[:, 0:MLA_ROPE_D] + k_rot_b * sin[:, 0:MLA_ROPE_D]
        q = _mm(cqn, w_q_ref[...])
        q_pe = (q[:, n_nope:n_nope + npe] * cos + q[:, n_nope + npe:] * sin) * MLA_SOFTMAX_SCALE
        for h in range(MLA_H):
            q_lat = _mm(q[:, h * MLA_NOPE_D:(h + 1) * MLA_NOPE_D], w_uk_ref[h]) * MLA_SOFTMAX_SCALE
            q_ref[h, :, 0:MLA_KV_RANK] = q_lat.astype(q_ref.dtype)
            q_ref[h, :, MLA_KV_RANK:MLA_QK] = q_pe[:, h * MLA_ROPE_D:(h + 1) * MLA_ROPE_D].astype(q_ref.dtype)

    kpe_ref[...] = kpe
    k_ref[:, 0:MLA_KV_RANK] = c_new.astype(k_ref.dtype)
    k_ref[:, MLA_KV_RANK:MLA_QK] = kpe.astype(k_ref.dtype)


def _in_proj(x2, hist8, wts, rope, tm, t, transposed_q):
    m, d = x2.shape
    b = m // t
    row = lambda i: (i, 0)
    npe = MLA_H * MLA_ROPE_D
    act_dtype = BF16 if transposed_q else F32
    if transposed_q:
        tiles_per_seq = t // tm
        seq_spec = pl.BlockSpec((1, SUBLANES, GDN_QKV), lambda i: (i // tiles_per_seq, 0, 0))
        scratch = [pltpu.VMEM((tm + SUBLANES, GDN_QKV), F32)]
    else:
        tiles_per_seq = 1
        seq_spec = pl.BlockSpec((tm // SUBLANES, SUBLANES, GDN_QKV), lambda i: (i, 0, 0))
        scratch = []
    out_shape = [
        jax.ShapeDtypeStruct((m, GDN_QKV), F32),
        jax.ShapeDtypeStruct((b, SUBLANES, GDN_QKV), F32),
        jax.ShapeDtypeStruct((m, GDN_QK), F32),
        jax.ShapeDtypeStruct((m, LANES), F32),
        jax.ShapeDtypeStruct((m, MLA_KV_RANK), F32),
        jax.ShapeDtypeStruct((m, MLA_ROPE_D), F32),
    ]
    out_specs = [
        pl.BlockSpec((tm, GDN_QKV), row), seq_spec, pl.BlockSpec((tm, GDN_QK), row),
        pl.BlockSpec((tm, LANES), row), pl.BlockSpec((tm, MLA_KV_RANK), row),
        pl.BlockSpec((tm, MLA_ROPE_D), row),
    ]
    if transposed_q:
        n_rope_tiles = rope[0].shape[1] // tm
        rope_specs = [pl.BlockSpec((npe, tm), lambda i: (0, i % n_rope_tiles))] * 2
        rope_specs += [pl.BlockSpec((tm, MLA_ROPE_D), lambda i: (i % n_rope_tiles, 0))] * 2
        w_q, w_uk = wts["w_q_t"], wts["w_uk"]
        out_shape += [jax.ShapeDtypeStruct((m // tm, MLA_QK, MLA_H * tm), act_dtype),
                      jax.ShapeDtypeStruct((m, MLA_QK), act_dtype),
                      jax.ShapeDtypeStruct((CT_ROWS, m), act_dtype)]
        out_specs += [pl.BlockSpec((None, MLA_QK, MLA_H * tm), lambda i: (i, 0, 0)),
                      pl.BlockSpec((tm, MLA_QK), row),
                      pl.BlockSpec((CT_ROWS, tm), lambda i: (0, i))]
    else:
        n_rope_tiles = rope[0].shape[0] // tm
        rope_specs = [pl.BlockSpec((tm, npe), lambda i: (i % n_rope_tiles, 0))] * 2
        w_q, w_uk = wts["w_q"], wts["w_uk_t"]
        out_shape += [jax.ShapeDtypeStruct((MLA_H, m, MLA_QK), act_dtype),
                      jax.ShapeDtypeStruct((m, MLA_QK), act_dtype)]
        out_specs += [pl.BlockSpec((MLA_H, tm, MLA_QK), lambda i: (0, i, 0)),
                      pl.BlockSpec((tm, MLA_QK), row)]
    return pl.pallas_call(
        functools.partial(_in_proj_kernel, transposed_q=transposed_q, tiles_per_seq=tiles_per_seq),
        grid=(m // tm,),
        in_specs=[
            pl.BlockSpec((tm, d), row),
            _const_spec(wts["w_qkv"].shape), _const_spec(wts["w_z"].shape),
            _const_spec(wts["w_rest"].shape), _const_spec(w_q.shape), _const_spec(w_uk.shape),
            _const_spec(wts["q_norm"].shape), _const_spec(wts["kv_norm"].shape),
            _const_spec(wts["gdn_conv_w"].shape), seq_spec,
            *rope_specs,
        ],
        out_specs=tuple(out_specs),
        out_shape=tuple(out_shape),
        scratch_shapes=scratch,
        compiler_params=pltpu.CompilerParams(
            dimension_semantics=("arbitrary",), vmem_limit_bytes=VMEM_LIMIT_BYTES),
        name="in_proj",
    )(x2, wts["w_qkv"], wts["w_z"], wts["w_rest"], w_q, w_uk, wts["q_norm"], wts["kv_norm"],
      wts["gdn_conv_w"], hist8, *rope)


def _gdn_kernel(qkv_ref, z_ref, small_ref, s0_ref, gpar_ref, normw_ref,
                tril_ref, bones_ref, eb_ref, eg_ref,
                o_ref, sout_ref, s_scr, o_buf, *, t_in, tt, n_seq):
    t = pl.program_id(1)
    c_rows, gw = GDN_CHUNK, GDN_GROUP
    padded = tt > t_in
    rows_all = n_seq * tt

    @pl.when(t == 0)
    def _():
        s_scr[...] = jnp.zeros(s_scr.shape, F32)
        for s in range(n_seq):
            for h in range(GDN_H):
                g, hh = divmod(h, HEADS_PER_GROUP)
                s_scr[s * GDN_GROUPS + g, hh * GDN_D:(hh + 1) * GDN_D,
                      hh * GDN_D:(hh + 1) * GDN_D] = s0_ref[s, h]

    def gather_rows(ref, width):
        parts = []
        for s in range(n_seq):
            parts.append(ref[s])
            if padded:
                parts.append(jnp.zeros((tt - t_in, width), F32))
        return parts[0] if len(parts) == 1 else jnp.concatenate(parts, axis=0)

    qkv = gather_rows(qkv_ref, GDN_QKV)
    small = gather_rows(small_ref, LANES)
    z = gather_rows(z_ref, GDN_QK)
    if padded:
        row_valid = lax.broadcasted_iota(jnp.int32, (rows_all, 1), 0) % tt < t_in
    beta_all = _sigmoid(small)
    xa = small + gpar_ref[1:2, :]
    softplus = jnp.maximum(xa, 0.0) + jnp.log1p(jnp.exp(-jnp.abs(xa)))
    g_all = -jnp.exp(gpar_ref[0:1, :]) * softplus
    if padded:
        beta_all = jnp.where(row_valid, beta_all, 0.0)
        g_all = jnp.where(row_valid, g_all, 0.0)

    def split3(x):
        hi = x.astype(BF16)
        r1 = x - hi.astype(F32)
        mid = r1.astype(BF16)
        return hi, mid, (r1 - mid.astype(F32)).astype(BF16)

    def mm_exact(x, ones_mat, terms=3):
        return sum(jnp.dot(part, ones_mat, preferred_element_type=F32) for part in split3(x)[:terms])

    def mm_exact_left(ones_mat, x):
        return sum(jnp.dot(ones_mat, part, preferred_element_type=F32) for part in split3(x))

    gc_all = mm_exact_left(tril_ref[...], g_all)
    beta_x = mm_exact(beta_all, eb_ref[...])
    gc_x = mm_exact(gc_all, eg_ref[...])
    block_ones = bones_ref[...]

    def block_diag(x):
        return jnp.concatenate([x.astype(BF16)] * HEADS_PER_GROUP, axis=0) * block_ones

    rr = lax.broadcasted_iota(jnp.int32, (c_rows, gw), 0)
    cc = lax.broadcasted_iota(jnp.int32, (c_rows, gw), 1) % c_rows
    incl, strict, diag = rr >= cc, rr > cc, rr == cc
    eye_t = diag.astype(F32)
    n_levels = int(math.log2(c_rows)) - 1
    n_chunks = rows_all // c_rows
    chunks_per_seq = tt // c_rows

    q_n, k_n = [], []
    for g in range(GDN_GROUPS):
        q_g = qkv[:, g * gw:(g + 1) * gw]
        k_g = qkv[:, GDN_QK + g * gw:GDN_QK + (g + 1) * gw]
        q_n.append(q_g * lax.rsqrt(mm_exact(q_g * q_g, block_ones, 2) + L2_EPS) * (GDN_D ** -0.5))
        k_n.append(k_g * lax.rsqrt(mm_exact(k_g * k_g, block_ones, 2) + L2_EPS))

    units = [(g, c) for c in range(n_chunks) for g in range(GDN_GROUPS)]
    st = {}
    for g, c in units:
        rows = slice(c * c_rows, (c + 1) * c_rows)
        lanes = slice(g * gw, (g + 1) * gw)
        q_c, k_c = q_n[g][rows], k_n[g][rows]
        v_c = qkv[rows, 2 * GDN_QK + g * gw:2 * GDN_QK + (g + 1) * gw]
        b_c = beta_x[rows, lanes]
        gc_c = gc_x[rows, lanes]
        g_row = jnp.sum(jnp.where(diag, gc_c, 0.0), axis=0, keepdims=True)
        g_last = gc_c[c_rows - 1:c_rows, :]
        decay = jnp.exp(jnp.where(incl, gc_c - g_row, -jnp.inf))
        e_col = jnp.exp(gc_c)
        kb = k_c * b_c
        kk_qk = lax.dot_general(jnp.concatenate([kb, q_c], axis=0).astype(BF16), block_diag(k_c),
                                _NT, preferred_element_type=F32)
        lmat = jnp.where(strict, kk_qk[0:c_rows] * decay, 0.0)
        st[g, c] = dict(
            inv=eye_t - lmat, lmat=lmat,
            a=(kk_qk[c_rows:] * decay).astype(BF16),
            vb=block_diag(v_c * b_c), kbe=block_diag(kb * e_col),
            qg=q_c * e_col,
            kg=(k_c * jnp.exp(g_last - gc_c)).astype(BF16),
            s_decay=jnp.exp(g_last),
        )
    for u in units:
        st[u]["pw"] = _mm(st[u]["lmat"], block_diag(st[u]["lmat"]))
    for lvl in range(n_levels):
        for u in units:
            s_u = st[u]
            bd = block_diag(s_u["pw"])
            if lvl < n_levels - 1:
                both = _mm(jnp.concatenate([s_u["pw"], s_u["inv"]], axis=0), bd)
                s_u["pw"] = both[0:c_rows]
                s_u["inv"] = s_u["inv"] + both[c_rows:]
            else:
                s_u["inv"] = s_u["inv"] + _mm(s_u["inv"], bd)
    for u in units:
        inv_b = st[u]["inv"].astype(BF16)
        st[u]["u"] = jnp.dot(inv_b, st[u]["vb"], preferred_element_type=F32)
        st[u]["w"] = jnp.dot(inv_b, st[u]["kbe"], preferred_element_type=F32)

    n_states = n_seq * GDN_GROUPS
    s_bd = [s_scr[i] for i in range(n_states)]
    block_ones_f = block_ones.astype(F32)
    for g, c in units:
        p = st[g, c]
        si = (c // chunks_per_seq) * GDN_GROUPS + g
        both = _mm(jnp.concatenate([p["w"], p["qg"]], axis=0), s_bd[si])
        v_new = p["u"] - both[0:c_rows]
        o_buf[c * c_rows:(c + 1) * c_rows, g * gw:(g + 1) * gw] = (
            both[c_rows:] + jnp.dot(p["a"], block_diag(v_new), preferred_element_type=F32))
        upd = lax.dot_general(p["kg"], v_new.astype(BF16), _TN, preferred_element_type=F32)
        s_bd[si] = s_bd[si] * p["s_decay"] + upd * block_ones_f
    for i in range(n_states):
        s_scr[i] = s_bd[i]

    norm_w = jnp.concatenate([normw_ref[...]] * HEADS_PER_GROUP, axis=1)
    outs = []
    for g in range(GDN_GROUPS):
        o_g = o_buf[:, g * gw:(g + 1) * gw]
        ms = mm_exact(o_g * o_g, block_ones, 2) * (1.0 / GDN_D)
        outs.append(o_g * lax.rsqrt(ms + RMS_EPS) * norm_w * _silu(z[:, g * gw:(g + 1) * gw]))
    o_all = jnp.concatenate(outs, axis=1)
    for s in range(n_seq):
        o_ref[s] = o_all[s * tt:s * tt + t_in].astype(o_ref.dtype)

    @pl.when(t == pl.num_programs(1) - 1)
    def _():
        for s in range(n_seq):
            for h in range(GDN_H):
                g, hh = divmod(h, HEADS_PER_GROUP)
                sout_ref[s, h] = s_scr[s * GDN_GROUPS + g, hh * GDN_D:(hh + 1) * GDN_D,
                                       hh * GDN_D:(hh + 1) * GDN_D]


def _gdn_constants(tt):
    tril = np.kron(np.eye(tt // GDN_CHUNK), np.tril(np.ones((GDN_CHUNK, GDN_CHUNK))))
    block_ones = np.kron(np.eye(HEADS_PER_GROUP), np.ones((GDN_D, GDN_D)))
    spread = np.kron(np.eye(GDN_H), np.ones((1, GDN_D)))
    e_b = np.zeros((LANES, GDN_QK))
    e_g = np.zeros((LANES, GDN_QK))
    e_b[0:GDN_H] = spread
    e_g[GDN_H:2 * GDN_H] = spread
    return [jnp.asarray(a, BF16) for a in (tril, block_ones, e_b, e_g)]


def _gdn(qkv, z, small, s0, gpar, norm_w, t_in, n_seq, act_dtype):
    b, t, _ = qkv.shape
    tt = max(t_in, GDN_CHUNK)
    consts = _gdn_constants(n_seq * tt)
    seq = lambda i, j: (i, j, 0)
    per_b4 = lambda i, j: (i, 0, 0, 0)
    return pl.pallas_call(
        functools.partial(_gdn_kernel, t_in=t_in, tt=tt, n_seq=n_seq),
        grid=(b // n_seq, t // t_in),
        in_specs=[
            pl.BlockSpec((n_seq, t_in, GDN_QKV), seq), pl.BlockSpec((n_seq, t_in, GDN_QK), seq),
            pl.BlockSpec((n_seq, t_in, LANES), seq),
            pl.BlockSpec((n_seq, GDN_H, GDN_D, GDN_D), per_b4),
            _const_spec(gpar.shape), _const_spec(norm_w.shape),
            *[_const_spec(c.shape) for c in consts],
        ],
        out_specs=(pl.BlockSpec((n_seq, t_in, GDN_QK), seq),
                   pl.BlockSpec((n_seq, GDN_H, GDN_D, GDN_D), per_b4)),
        out_shape=(jax.ShapeDtypeStruct((b, t, GDN_QK), act_dtype),
                   jax.ShapeDtypeStruct((b, GDN_H, GDN_D, GDN_D), F32)),
        scratch_shapes=[pltpu.VMEM((n_seq * GDN_GROUPS, GDN_GROUP, GDN_GROUP), F32),
                        pltpu.VMEM((n_seq * tt, GDN_QK), F32)],
        compiler_params=pltpu.CompilerParams(
            dimension_semantics=("parallel", "arbitrary"), vmem_limit_bytes=VMEM_LIMIT_BYTES),
        name="gdn",
    )(qkv, z, small, s0, gpar, norm_w, *consts)


def _attn_prompt_kernel(qi_ref, kj_ref, qt_ref, k_ref, ct_ref, w_uvt_ref, o_ref, m_scr, acc_scr,
                        *, tq, tk):
    p = pl.program_id(1)
    i = qi_ref[p]
    j = kj_ref[p]
    kv_per_q = tq // tk
    first_diag = i * kv_per_q

    @pl.when(j == 0)
    def _():
        m_scr[...] = jnp.full(m_scr.shape, -jnp.inf, F32)
        acc_scr[...] = jnp.zeros(acc_scr.shape, F32)

    def update(masked):
        s = jnp.dot(k_ref[...], qt_ref[...], preferred_element_type=F32)
        if masked:
            key = lax.broadcasted_iota(jnp.int32, s.shape, 0) + (j - first_diag) * tk
            qry = lax.broadcasted_iota(jnp.int32, s.shape, 1) % tq
            s = jnp.where(key <= qry, s, -jnp.inf)
        m_prev = m_scr[...]
        m_new = jnp.maximum(m_prev, jnp.max(s, axis=0, keepdims=True))
        alpha = jnp.exp2(m_prev - m_new)
        pmat = jnp.exp2(s - m_new)
        acc_scr[...] = alpha * acc_scr[...] + jnp.dot(ct_ref[...], pmat.astype(BF16),
                                                      preferred_element_type=F32)
        m_scr[...] = m_new

    @pl.when(j < first_diag)
    def _():
        update(False)

    @pl.when(j >= first_diag)
    def _():
        update(True)

    @pl.when(j == first_diag + kv_per_q - 1)
    def _():
        o_lat_t = acc_scr[0:MLA_KV_RANK, :] / acc_scr[MLA_KV_RANK:MLA_KV_RANK + 1, :]
        outs = [_mm(w_uvt_ref[h], o_lat_t[:, h * tq:(h + 1) * tq]) for h in range(MLA_H)]
        o_ref[...] = jnp.concatenate(outs, axis=0).T.astype(o_ref.dtype)


def _attn_prompt(q_t, k, c_t, w_uv_t, b, t, tq, tk):
    nq, nk = t // tq, t // tk
    kv_per_q = tq // tk
    m = b * t
    assert q_t.shape == (m // tq, MLA_QK, MLA_H * tq), "query tiles must match the projection tiles"
    pairs = [(i, j) for i in range(nq) for j in range((i + 1) * kv_per_q)]
    qi = jnp.asarray(np.array([p[0] for p in pairs], np.int32))
    kj = jnp.asarray(np.array([p[1] for p in pairs], np.int32))
    grid_spec = pltpu.PrefetchScalarGridSpec(
        num_scalar_prefetch=2,
        grid=(b, len(pairs)),
        in_specs=[
            pl.BlockSpec((None, MLA_QK, MLA_H * tq), lambda bi, p, qi, kj: (bi * nq + qi[p], 0, 0)),
            pl.BlockSpec((tk, MLA_QK), lambda bi, p, qi, kj: (bi * nk + kj[p], 0)),
            pl.BlockSpec((CT_ROWS, tk), lambda bi, p, qi, kj: (0, bi * nk + kj[p])),
            pl.BlockSpec(w_uv_t.shape, lambda bi, p, qi, kj: (0, 0, 0)),
        ],
        out_specs=pl.BlockSpec((tq, MLA_H * MLA_V_D), lambda bi, p, qi, kj: (bi * nq + qi[p], 0)),
        scratch_shapes=[pltpu.VMEM((1, MLA_H * tq), F32),
                        pltpu.VMEM((CT_ROWS, MLA_H * tq), F32)],
    )
    return pl.pallas_call(
        functools.partial(_attn_prompt_kernel, tq=tq, tk=tk),
        grid_spec=grid_spec,
        out_shape=jax.ShapeDtypeStruct((m, MLA_H * MLA_V_D), BF16),
        compiler_params=pltpu.CompilerParams(
            dimension_semantics=("parallel", "arbitrary"), vmem_limit_bytes=VMEM_LIMIT_BYTES),
        name="attn_prompt",
    )(qi, kj, q_t, k, c_t, w_uv_t)


def _softmax_update(s, v_b, m_scr, l_scr, acc_scr):
    m_prev = m_scr[...]
    m_new = jnp.maximum(m_prev, jnp.max(s, axis=-1, keepdims=True))
    alpha = jnp.exp(m_prev - m_new)
    p = jnp.exp(s - m_new)
    l_scr[...] = alpha * l_scr[...] + jnp.sum(p, axis=-1, keepdims=True)
    acc_scr[...] = alpha * acc_scr[...] + jnp.dot(p.astype(BF16), v_b, preferred_element_type=F32)
    m_scr[...] = m_new


def _attn_sample_kernel(pt_ref, q_ref, knew_ref, w_uv_ref, ckv_hbm, kpe_hbm, o_ref,
                        c_buf, kpe_buf, sems, m_scr, l_scr, acc_scr, *,
                        layer, pages_per_step, page, t_new):
    seq = pl.program_id(0)
    grp = pl.program_id(1)
    n_grp = pl.num_programs(1)
    step = seq * n_grp + grp
    slot = step % 2
    rows = MLA_H * t_new

    def page_copies(seq_i, grp_i, slot_i, p):
        pid = pt_ref[seq_i, grp_i * pages_per_step + p]
        return (
            pltpu.make_async_copy(ckv_hbm.at[layer, pid],
                                  c_buf.at[slot_i, pl.ds(p * page, page), :], sems.at[0, slot_i]),
            pltpu.make_async_copy(kpe_hbm.at[layer, pid],
                                  kpe_buf.at[slot_i, :, pl.ds(p * page, page)], sems.at[1, slot_i]),
        )

    def start_fetch(seq_i, grp_i, slot_i):
        for p in range(pages_per_step):
            for cp in page_copies(seq_i, grp_i, slot_i, p):
                cp.start()

    @pl.when(step == 0)
    def _():
        start_fetch(seq, grp, slot)

    @pl.when(step + 1 < pl.num_programs(0) * n_grp)
    def _():
        nxt = step + 1
        start_fetch(nxt // n_grp, nxt % n_grp, 1 - slot)

    @pl.when(grp == 0)
    def _():
        m_scr[...] = jnp.full(m_scr.shape, -jnp.inf, F32)
        l_scr[...] = jnp.zeros(l_scr.shape, F32)
        acc_scr[...] = jnp.zeros(acc_scr.shape, F32)

    for p in range(pages_per_step):
        for cp in page_copies(seq, grp, slot, p):
            cp.wait()

    q = q_ref[...].reshape(rows, MLA_QK).astype(BF16)
    c_b = c_buf[slot].astype(BF16)
    s = (lax.dot_general(q[:, 0:MLA_KV_RANK], c_b, _NT, preferred_element_type=F32)
         + jnp.dot(q[:, MLA_KV_RANK:MLA_QK], kpe_buf[slot].astype(BF16), preferred_element_type=F32))
    _softmax_update(s, c_b, m_scr, l_scr, acc_scr)

    @pl.when(grp == n_grp - 1)
    def _():
        k_new = knew_ref[...]
        s = _mm_nt(q, k_new)
        qi = lax.broadcasted_iota(jnp.int32, (rows, t_new), 0) % t_new
        ki = lax.broadcasted_iota(jnp.int32, (rows, t_new), 1)
        s = jnp.where(ki <= qi, s, -jnp.inf)
        _softmax_update(s, k_new[:, 0:MLA_KV_RANK].astype(BF16), m_scr, l_scr, acc_scr)
        o_lat = acc_scr[...] / l_scr[...]
        outs = [_mm(o_lat[h * t_new:(h + 1) * t_new], w_uv_ref[h]) for h in range(MLA_H)]
        o_ref[...] = jnp.concatenate(outs, axis=-1).astype(o_ref.dtype)


def _attn_sample(q, k_new, cache_ckv, cache_kpe, layer, page_table, w_uv, b, t_new):
    cache_kpe = jnp.swapaxes(cache_kpe, 2, 3)
    n_pages = page_table.shape[1]
    page = cache_ckv.shape[2]
    pages_per_step = math.gcd(SAMPLE_PAGES_PER_STEP, n_pages)
    rows = MLA_H * t_new
    n_steps = n_pages // pages_per_step

    keys = pages_per_step * page
    grid_spec = pltpu.PrefetchScalarGridSpec(
        num_scalar_prefetch=1,
        grid=(b, n_steps),
        in_specs=[
            pl.BlockSpec((MLA_H, t_new, MLA_QK), lambda bi, s, pt: (0, bi, 0)),
            pl.BlockSpec((t_new, MLA_QK), lambda bi, s, pt: (bi, 0)),
            pl.BlockSpec(w_uv.shape, lambda bi, s, pt: (0, 0, 0)),
            pl.BlockSpec(memory_space=pl.ANY),
            pl.BlockSpec(memory_space=pl.ANY),
        ],
        out_specs=pl.BlockSpec((t_new, MLA_H * MLA_V_D), lambda bi, s, pt: (bi, 0)),
        scratch_shapes=[pltpu.VMEM((2, keys, MLA_KV_RANK), F32),
                        pltpu.VMEM((2, MLA_ROPE_D, keys), F32),
                        pltpu.SemaphoreType.DMA((2, 2)),
                        pltpu.VMEM((rows, 1), F32), pltpu.VMEM((rows, 1), F32),
                        pltpu.VMEM((rows, MLA_KV_RANK), F32)],
    )
    return pl.pallas_call(
        functools.partial(_attn_sample_kernel, layer=layer, pages_per_step=pages_per_step,
                          page=page, t_new=t_new),
        grid_spec=grid_spec,
        out_shape=jax.ShapeDtypeStruct((b * t_new, MLA_H * MLA_V_D), F32),
        compiler_params=pltpu.CompilerParams(
            dimension_semantics=("arbitrary", "arbitrary"), vmem_limit_bytes=VMEM_LIMIT_BYTES),
        name="attn_sample",
    )(page_table, q, k_new, w_uv, cache_ckv, cache_kpe)


def _ffn_kernel(x_ref, og_ref, om_ref, hist_ref, w_out_ref, ln_ref, w_up_ref, cw_ref, w_dn_ref,
                y_ref, nf_ref, *scratch, tm, alpha, short_seq):
    d_gdn = og_ref.shape[-1]
    n_chunks, _, two_fc = w_up_ref.shape
    fc = two_fc // 2
    mix = _mm(og_ref[...], w_out_ref[0:d_gdn, :]) + _mm(om_ref[...], w_out_ref[d_gdn:, :])
    x1 = _layernorm(alpha * x_ref[...] + mix, ln_ref[0:1, :], ln_ref[1:2, :])
    x1_b = x1.astype(BF16)

    if short_seq:
        h_buf, = scratch
        nb = tm // SUBLANES
        t_pos = lax.broadcasted_iota(jnp.int32, (nb, SUBLANES, two_fc), 1)
    else:
        h_buf, ubuf, carry = scratch

        @pl.when(pl.program_id(1) == 0)
        def _():
            carry[...] = hist_ref[0]

    for c in range(n_chunks):
        cols = slice(c * two_fc, (c + 1) * two_fc)
        up = jnp.dot(x1_b, w_up_ref[c], preferred_element_type=F32)
        cw = cw_ref[c]
        if short_seq:
            up3 = up.reshape(nb, SUBLANES, two_fc)
            h0 = hist_ref[:, SUBLANES - 2:SUBLANES - 1, cols]
            h1 = hist_ref[:, SUBLANES - 1:SUBLANES, cols]
            prev1 = jnp.where(t_pos == 0, h1, pltpu.roll(up3, 1, axis=1))
            prev2 = jnp.where(t_pos == 0, h0, jnp.where(t_pos == 1, h1, pltpu.roll(up3, 2, axis=1)))
            hc = (prev2 * cw[0:1] + prev1 * cw[1:2] + up3 * cw[2:3]).reshape(tm, two_fc)
            nf_ref[:, :, cols] = up3
        else:
            ubuf[0:SUBLANES, :] = carry[:, cols]
            ubuf[SUBLANES:SUBLANES + tm, :] = up
            hc = (ubuf[pl.ds(SUBLANES - 2, tm), :] * cw[0:1]
                  + ubuf[pl.ds(SUBLANES - 1, tm), :] * cw[1:2] + up * cw[2:3])
            last = ubuf[tm:tm + SUBLANES, :]
            carry[:, cols] = last
            nf_ref[0, :, cols] = last
        h_buf[:, c * fc:(c + 1) * fc] = (_silu(hc[:, fc:]) * hc[:, 0:fc]).astype(BF16)
    down = jnp.dot(h_buf[...], w_dn_ref[...], preferred_element_type=F32)
    y_ref[...] = _layernorm(alpha * x1 + down, ln_ref[2:3, :], ln_ref[3:4, :])


def _ffn(x2, o_gdn, o_mla, hist8, wts, b, t, tm, alpha):
    m, d = x2.shape
    n_chunks, _, two_fc = wts["w_up"].shape
    two_ff = n_chunks * two_fc
    short_seq = t == SUBLANES
    if short_seq:
        grid = (m // tm, 1)
        nb = tm // SUBLANES
        row = lambda i, j: (i, 0)
        hist_spec = pl.BlockSpec((nb, SUBLANES, two_ff), lambda i, j: (i, 0, 0))
        scratch = [pltpu.VMEM((tm, two_ff // 2), BF16)]
    else:
        nt = t // tm
        grid = (b, nt)
        row = lambda i, j: (i * nt + j, 0)
        hist_spec = pl.BlockSpec((1, SUBLANES, two_ff), lambda i, j: (i, 0, 0))
        scratch = [pltpu.VMEM((tm, two_ff // 2), BF16),
                   pltpu.VMEM((tm + SUBLANES, two_fc), F32), pltpu.VMEM((SUBLANES, two_ff), F32)]
    return pl.pallas_call(
        functools.partial(_ffn_kernel, tm=tm, alpha=alpha, short_seq=short_seq),
        grid=grid,
        in_specs=[
            pl.BlockSpec((tm, d), row), pl.BlockSpec((tm, o_gdn.shape[1]), row),
            pl.BlockSpec((tm, o_mla.shape[1]), row), hist_spec,
            _const_spec(wts["w_out"].shape), _const_spec(wts["ln"].shape),
            _const_spec(wts["w_up"].shape), _const_spec(wts["ffn_cw"].shape),
            _const_spec(wts["w_dn"].shape),
        ],
        out_specs=(pl.BlockSpec((tm, d), row), hist_spec),
        out_shape=(jax.ShapeDtypeStruct((m, d), F32),
                   jax.ShapeDtypeStruct((b, SUBLANES, two_ff), F32)),
        scratch_shapes=scratch,
        compiler_params=pltpu.CompilerParams(
            dimension_semantics=("parallel", "arbitrary"), vmem_limit_bytes=VMEM_LIMIT_BYTES),
        name="ffn",
    )(x2, o_gdn, o_mla, hist8, wts["w_out"], wts["ln"], wts["w_up"], wts["ffn_cw"], wts["w_dn"])


def _swap_halves(w):
    half = w.shape[-1] // 2
    return jnp.concatenate([-w[..., half:], w[..., :half]], axis=-1)


def _prep_weights(w_in, gdn_conv_w, gdn_A_log, gdn_dt_bias, gdn_norm_w, mla_q_norm_w, mla_w_uq,
                  mla_kv_norm_w, mla_w_uk, mla_w_uv, w_out, ln1_g, ln1_b, ffn_w_up, ffn_conv_w,
                  ffn_w_down, ln2_g, ln2_b):
    d_model = w_in.shape[0]
    o = 0
    w_qkv = w_in[:, o:o + GDN_QKV]; o += GDN_QKV
    w_z = w_in[:, o:o + GDN_QK]; o += GDN_QK
    w_b = w_in[:, o:o + GDN_H]; o += GDN_H
    w_a = w_in[:, o:o + GDN_H]; o += GDN_H
    w_cq = w_in[:, o:o + MLA_Q_RANK]; o += MLA_Q_RANK
    w_ckv = w_in[:, o:o + MLA_KV_RANK]; o += MLA_KV_RANK
    w_kpe = w_in[:, o:o + MLA_ROPE_D]
    pad = jnp.zeros((d_model, LANES - 2 * GDN_H - 2 * MLA_ROPE_D), w_in.dtype)
    w_small = jnp.concatenate([w_b, w_a, w_kpe, _swap_halves(w_kpe), pad], axis=1)
    w_rest = jnp.concatenate([w_cq, w_ckv, w_small], axis=1)

    uq = mla_w_uq.reshape(MLA_Q_RANK, MLA_H, MLA_NOPE_D + MLA_ROPE_D)
    uq_nope = uq[:, :, :MLA_NOPE_D].reshape(MLA_Q_RANK, MLA_H * MLA_NOPE_D)
    uq_pe = uq[:, :, MLA_NOPE_D:]
    w_q = jnp.concatenate([uq_nope, uq_pe.reshape(MLA_Q_RANK, -1),
                           _swap_halves(uq_pe).reshape(MLA_Q_RANK, -1)], axis=1)

    gpar = jnp.zeros((2, LANES), F32)
    gpar = gpar.at[0, GDN_H:2 * GDN_H].set(gdn_A_log).at[1, GDN_H:2 * GDN_H].set(gdn_dt_bias)

    d_ff = ffn_w_down.shape[0]
    n_chunks = d_ff // FFN_CHUNK

    def interleave(w):
        lead = w.shape[:-1]
        w = w.reshape(lead + (2, n_chunks, FFN_CHUNK))
        return jnp.moveaxis(w, -3, -2).reshape(lead + (n_chunks, 2 * FFN_CHUNK))

    return {
        "w_qkv": w_qkv.astype(BF16), "w_z": w_z.astype(BF16), "w_rest": w_rest.astype(BF16),
        "w_q": w_q.astype(BF16), "w_q_t": w_q.T.astype(BF16),
        "w_uk_t": jnp.transpose(mla_w_uk, (1, 2, 0)).astype(BF16),
        "w_uk": jnp.transpose(mla_w_uk, (1, 0, 2)).astype(BF16),
        "w_uv": jnp.transpose(mla_w_uv, (1, 0, 2)).astype(BF16),
        "w_uv_t": jnp.transpose(mla_w_uv, (1, 2, 0)).astype(BF16),
        "q_norm": mla_q_norm_w.reshape(1, -1), "kv_norm": mla_kv_norm_w.reshape(1, -1),
        "gdn_conv_w": gdn_conv_w, "gpar": gpar, "gdn_norm_w": gdn_norm_w.reshape(1, -1),
        "w_out": w_out.astype(BF16),
        "ln": jnp.stack([ln1_g, ln1_b, ln2_g, ln2_b]),
        "w_up": jnp.transpose(interleave(ffn_w_up), (1, 0, 2)).astype(BF16),
        "ffn_cw": jnp.transpose(interleave(ffn_conv_w), (1, 0, 2)),
        "w_dn": ffn_w_down.astype(BF16),
        "interleave": interleave, "n_chunks": n_chunks,
    }


def _rope_tables(past_len, t, tm, transposed):
    half = MLA_ROPE_D // 2
    pos = (past_len + jnp.arange(t, dtype=jnp.int32)).astype(F32)
    inv = ROPE_BASE ** (-jnp.arange(half, dtype=F32) / half)
    ang = pos[:, None] * inv
    reps = (max(tm // t, 1), 2 * MLA_H)
    cos, sin = jnp.tile(jnp.cos(ang), reps), jnp.tile(jnp.sin(ang), reps)
    if transposed:
        return cos.T, sin.T, cos[:, 0:MLA_ROPE_D], sin[:, 0:MLA_ROPE_D]
    return cos, sin


def _pad_hist(h):
    return jnp.pad(h, ((0, 0), (SUBLANES - h.shape[1], 0), (0, 0)))


def _tiles(b, t):
    if t == SUBLANES:
        rows = min(256, b * t)
        return dict(proj=rows, ffn=rows, gdn=t, gdn_seqs=math.gcd(4, b), attn_q=None, attn_k=None)
    return dict(proj=min(512, t), ffn=min(512, t), gdn=min(512, t), gdn_seqs=1,
                attn_q=min(512, t), attn_k=min(512, t))


def _layer(x, past, layer, s0, conv_hist, ffn_hist, wts, alpha):
    b, t, d = x.shape
    m = b * t
    x2 = x.reshape(m, d)
    tiles = _tiles(b, t)
    short_seq = past is not None
    past_len = past[2].shape[1] * past[0].shape[2] if short_seq else 0
    act_dtype = F32 if short_seq else BF16
    rope = _rope_tables(past_len, t, tiles["proj"], transposed=not short_seq)
    qkv, conv_tail, z, small, c_new, kpe_new, q, k, *c_t = _in_proj(
        x2, _pad_hist(conv_hist), wts, rope, tiles["proj"], t, transposed_q=not short_seq)

    o_gdn, s_new = _gdn(qkv.reshape(b, t, -1), z.reshape(b, t, -1), small.reshape(b, t, -1),
                        s0, wts["gpar"], wts["gdn_norm_w"], tiles["gdn"], tiles["gdn_seqs"], act_dtype)
    if short_seq:
        cache_ckv, cache_kpe, page_table = past
        o_mla = _attn_sample(q, k, cache_ckv, cache_kpe, layer, page_table, wts["w_uv"], b, t)
    else:
        o_mla = _attn_prompt(q, k, c_t[0], wts["w_uv_t"], b, t, tiles["attn_q"], tiles["attn_k"])

    hist8 = wts["interleave"](_pad_hist(ffn_hist)).reshape(b, SUBLANES, -1)
    y, nf8 = _ffn(x2, o_gdn.reshape(m, -1), o_mla, hist8, wts, b, t, tiles["ffn"], alpha)

    n_chunks = wts["n_chunks"]
    nf = nf8[:, SUBLANES - (FFN_CONV_TAPS - 1):, :].reshape(b, FFN_CONV_TAPS - 1, n_chunks, 2, FFN_CHUNK)
    new_ffn = jnp.moveaxis(nf, -2, -3).reshape(b, FFN_CONV_TAPS - 1, -1)
    new_conv = conv_tail[:, SUBLANES - (GDN_CONV_TAPS - 1):, :]
    return (y.reshape(b, t, d), c_new.reshape(b, t, -1), kpe_new.reshape(b, t, -1), s_new,
            new_conv, new_ffn)


def kernel(x_prompt, x_sample, cache_ckv, cache_kpe, page_table, state_gdn, state_gdn_conv,
           state_ffn_conv, w_in, gdn_conv_w, gdn_A_log, gdn_dt_bias, gdn_norm_w, mla_q_norm_w,
           mla_w_uq, mla_kv_norm_w, mla_w_uk, mla_w_uv, w_out, ln1_g, ln1_b, ffn_w_up, ffn_conv_w,
           ffn_w_down, ln2_g, ln2_b):
    depth = w_in.shape[0]
    alpha = (2.0 * depth) ** 0.25
    bp, tp, _ = x_prompt.shape
    bs, ts, _ = x_sample.shape
    assert ts == SUBLANES, "sample group: one 8-row tile per sequence"
    xp, xs = x_prompt, x_sample
    new_p, new_s = [], []
    for l in range(depth):
        wts = _prep_weights(w_in[l], gdn_conv_w[l], gdn_A_log[l], gdn_dt_bias[l], gdn_norm_w[l],
                            mla_q_norm_w[l], mla_w_uq[l], mla_kv_norm_w[l], mla_w_uk[l],
                            mla_w_uv[l], w_out[l], ln1_g[l], ln1_b[l], ffn_w_up[l], ffn_conv_w[l],
                            ffn_w_down[l], ln2_g[l], ln2_b[l])
        two_ff = ffn_w_up.shape[-1]
        xp, *st_p = _layer(
            xp, None, l, jnp.zeros((bp, GDN_H, GDN_D, GDN_D), F32),
            jnp.zeros((bp, GDN_CONV_TAPS - 1, GDN_QKV), F32),
            jnp.zeros((bp, FFN_CONV_TAPS - 1, two_ff), F32), wts, alpha)
        xs, *st_s = _layer(
            xs, (cache_ckv, cache_kpe, page_table), l, state_gdn[l], state_gdn_conv[l],
            state_ffn_conv[l], wts, alpha)
        new_p.append(st_p)
        new_s.append(st_s)
    stack = lambda sts: [jnp.stack(v) for v in zip(*sts)]
    return (xp, xs, *stack(new_p), *stack(new_s))
```
